```python
import math
import jax, jax.numpy as jnp
from jax import lax
import numpy as np

D_MODEL = 2048
BATCH = 1
SEQ = 8192
DEPTH = 1
DEC_BATCH = 32
DEC_SEQ = 1
PAST_LEN = 8192
PAGE_SIZE = 128

D_MIX = D_MODEL
D_CONV = D_MIX // 2
D_ATT = D_MIX - D_CONV
N_HEADS = 8
HEAD_DIM = D_ATT // N_HEADS
DK = HEAD_DIM // 2
CONV_WIDTH = 31
CONV_STATE = CONV_WIDTH - 1
Q_BLOCK = 128
EPS = 1e-6
D_IN = 3 * D_CONV + 4 * D_ATT

kernel_name = "hymba_conformer_diffattn_step"


def rms_norm(x, g):
    xf = x.astype(jnp.float32)
    y = xf * lax.rsqrt(jnp.mean(xf * xf, axis=-1, keepdims=True) + EPS)
    return (y * g.astype(jnp.float32)).astype(x.dtype)


def layer_norm(x, g, b):
    xf = x.astype(jnp.float32)
    mu = jnp.mean(xf, axis=-1, keepdims=True)
    d = xf - mu
    var = jnp.mean(d * d, axis=-1, keepdims=True)
    y = d * lax.rsqrt(var + EPS)
    return (y * g.astype(jnp.float32) + b.astype(jnp.float32)).astype(x.dtype)


def lambda_init(layer_idx):
    return 0.8 - 0.6 * math.exp(-0.3 * layer_idx)


def diff_attend(q, k, v, mask, lam):
    s = jnp.einsum('bqhmd,bkhmd->bhmqk', q, k, preferred_element_type=jnp.float32) * (DK ** -0.5)
    s = jnp.where(mask[None, None, None], s, -jnp.inf)
    p = jax.nn.softmax(s, axis=-1)
    a = p[:, :, 0] - lam * p[:, :, 1]
    o = jnp.einsum('bhqk,bkhd->bqhd', a, v.astype(jnp.float32))
    return o.astype(v.dtype)


def prompt_attend(q, k, v, lam):
    B, S = q.shape[0], q.shape[1]
    nb = S // Q_BLOCK
    qb = q.reshape(B, nb, Q_BLOCK, N_HEADS, 2, DK).swapaxes(0, 1)
    pos_k = jnp.arange(S)

    def block(args):
        qi, i = args
        pos_q = i * Q_BLOCK + jnp.arange(Q_BLOCK)
        return diff_attend(qi, k, v, pos_k[None, :] <= pos_q[:, None], lam)

    o = lax.map(block, (qb, jnp.arange(nb)))
    return o.swapaxes(0, 1).reshape(B, S, N_HEADS, HEAD_DIM)


def mixer_layer(x, c, conv_prefix, attend, layer_idx, w_ada, b_ada, g_pre, g_post, w_in,
                conv_w, conv_b, g_cn, b_cn, lq1, lk1, lq2, lk2, g_subln, w_out):
    B, T, _ = x.shape
    mod = jax.nn.silu(c) @ w_ada + b_ada
    shift, scale, gate = jnp.split(mod, 3, axis=-1)
    h = rms_norm(x, g_pre) * (1 + scale[:, None]) + shift[:, None]
    proj = h @ w_in
    a_glu, b_glu, z_conv, q, k, v, z_att = jnp.split(
        proj, [D_CONV, 2 * D_CONV, 3 * D_CONV, 3 * D_CONV + D_ATT,
               3 * D_CONV + 2 * D_ATT, 3 * D_CONV + 3 * D_ATT], axis=-1)

    u = a_glu * jax.nn.sigmoid(b_glu)
    buf = jnp.concatenate([conv_prefix.astype(u.dtype), u], axis=1)
    conv = lax.conv_general_dilated(
        buf, conv_w[:, None, :].astype(buf.dtype), (1,), 'VALID',
        dimension_numbers=('NWC', 'WIO', 'NWC'), feature_group_count=D_CONV) + conv_b
    out_conv = jax.nn.silu(layer_norm(conv, g_cn, b_cn)) * jax.nn.silu(z_conv)
    new_conv = buf[:, -CONV_STATE:]

    lam = (jnp.exp(jnp.sum(lq1.astype(jnp.float32) * lk1.astype(jnp.float32)))
           - jnp.exp(jnp.sum(lq2.astype(jnp.float32) * lk2.astype(jnp.float32)))
           + lambda_init(layer_idx))
    qh = q.reshape(B, T, N_HEADS, 2, DK)
    kh = k.reshape(B, T, N_HEADS, 2, DK)
    vh = v.reshape(B, T, N_HEADS, HEAD_DIM)
    o = attend(qh, kh, vh, lam)
    o = rms_norm(o, g_subln) * (1 - lambda_init(layer_idx))
    out_att = o.reshape(B, T, D_ATT) * jax.nn.silu(z_att)

    y = jnp.concatenate([out_conv, out_att], axis=-1) @ w_out
    x_new = x + gate[:, None] * rms_norm(y, g_post)
    return x_new, kh.reshape(B, T, N_HEADS, HEAD_DIM), vh, new_conv


def setup_inputs(seed: int = 0) -> dict:
    key = jax.random.key(seed)
    ks = jax.random.split(key, 24)
    n_pages = PAST_LEN // PAGE_SIZE
    used = DEC_BATCH * n_pages
    n_pool = used + max(1, used // 4)
    nrm = jax.random.normal
    f32 = jnp.float32
    page_table = jax.random.permutation(ks[0], n_pool)[:used].reshape(DEC_BATCH, n_pages).astype(jnp.int32)
    return {
        "x_prompt": nrm(ks[1], (BATCH, SEQ, D_MODEL), f32),
        "x_sample": nrm(ks[2], (DEC_BATCH, DEC_SEQ, D_MODEL), f32),
        "cache_k": nrm(ks[3], (DEPTH, n_pool, PAGE_SIZE, N_HEADS, HEAD_DIM), f32),
        "cache_v": nrm(ks[4], (DEPTH, n_pool, PAGE_SIZE, N_HEADS, HEAD_DIM), f32),
        "state_conv": nrm(ks[5], (DEPTH, DEC_BATCH, CONV_STATE, D_CONV), f32),
        "page_table": page_table,
        "c_prompt": nrm(ks[6], (BATCH, D_MODEL), f32),
        "c_sample": nrm(ks[7], (DEC_BATCH, D_MODEL), f32),
        "w_ada": nrm(ks[8], (DEPTH, D_MODEL, 3 * D_MODEL), f32) * (0.5 * D_MODEL ** -0.5),
        "b_ada": nrm(ks[9], (DEPTH, 3 * D_MODEL), f32) * 0.02,
        "g_pre": 1.0 + 0.02 * nrm(ks[10], (DEPTH, D_MODEL), f32),
        "g_post": 1.0 + 0.02 * nrm(ks[11], (DEPTH, D_MODEL), f32),
        "w_in": nrm(ks[12], (DEPTH, D_MODEL, D_IN), f32) * (D_MODEL ** -0.5),
        "conv_w": nrm(ks[13], (DEPTH, CONV_WIDTH, D_CONV), f32) * (CONV_WIDTH ** -0.5),
        "conv_b": nrm(ks[14], (DEPTH, D_CONV), f32) * 0.02,
        "g_cn": 1.0 + 0.02 * nrm(ks[15], (DEPTH, D_CONV), f32),
        "b_cn": nrm(ks[16], (DEPTH, D_CONV), f32) * 0.02,
        "lq1": nrm(ks[17], (DEPTH, DK), f32) * 0.1,
        "lk1": nrm(ks[18], (DEPTH, DK), f32) * 0.1,
        "lq2": nrm(ks[19], (DEPTH, DK), f32) * 0.1,
        "lk2": nrm(ks[20], (DEPTH, DK), f32) * 0.1,
        "g_subln": 1.0 + 0.02 * nrm(ks[21], (DEPTH, HEAD_DIM), f32),
        "w_out": nrm(ks[22], (DEPTH, D_MIX, D_MODEL), f32) * (D_MIX ** -0.5),
    }


def reference(x_prompt, x_sample, cache_k, cache_v, state_conv, page_table, c_prompt, c_sample,
              w_ada, b_ada, g_pre, g_post, w_in, conv_w, conv_b, g_cn, b_cn,
              lq1, lk1, lq2, lk2, g_subln, w_out):
    db = x_sample.shape[0]
    past = page_table.shape[1] * PAGE_SIZE
    xp, xs = x_prompt, x_sample
    kp_l, vp_l, cp_l, ks_l, vs_l, cs_l = [], [], [], [], [], []
    for l in range(DEPTH):
        w = (w_ada[l], b_ada[l], g_pre[l], g_post[l], w_in[l], conv_w[l], conv_b[l],
             g_cn[l], b_cn[l], lq1[l], lk1[l], lq2[l], lk2[l], g_subln[l], w_out[l])

        prefix0 = jnp.zeros((xp.shape[0], CONV_STATE, D_CONV), xp.dtype)
        xp, kp, vp, cp = mixer_layer(xp, c_prompt, prefix0, prompt_attend, l, *w)

        k_past = cache_k[l][page_table].reshape(db, past, N_HEADS, 2, DK)
        v_past = cache_v[l][page_table].reshape(db, past, N_HEADS, HEAD_DIM)

        def sample_attend(q, k, v, lam, k_past=k_past, v_past=v_past):
            k_all = jnp.concatenate([k_past.astype(k.dtype), k], axis=1)
            v_all = jnp.concatenate([v_past.astype(v.dtype), v], axis=1)
            t = q.shape[1]
            pos_k = jnp.arange(past + t)
            pos_q = past + jnp.arange(t)
            return diff_attend(q, k_all, v_all, pos_k[None, :] <= pos_q[:, None], lam)

        xs, ksn, vsn, csn = mixer_layer(xs, c_sample, state_conv[l], sample_attend, l, *w)
        kp_l.append(kp); vp_l.append(vp); cp_l.append(cp)
        ks_l.append(ksn); vs_l.append(vsn); cs_l.append(csn)

    k_prompt = jnp.stack(kp_l); v_prompt = jnp.stack(vp_l); conv_prompt = jnp.stack(cp_l)
    k_sample = jnp.stack(ks_l); v_sample = jnp.stack(vs_l); conv_sample = jnp.stack(cs_l)
    return (xp, xs, k_prompt, v_prompt, conv_prompt, k_sample, v_sample, conv_sample)
```

```python
import functools
import math

import jax
import jax.numpy as jnp
from jax import lax
from jax.experimental import pallas as pl
from jax.experimental.pallas import tpu as pltpu

F32 = jnp.float32
BF16 = jnp.bfloat16

N_HEADS = 8
CONV_WIDTH = 31
CONV_STATE = CONV_WIDTH - 1
PAGE_SIZE = 128
EPS = 1e-6

LANES = 128
VMEM_LIMIT = 56 * 1024 * 1024

NT_DIMS = (((1,), (1,)), ((), ()))
TN_DIMS = (((0,), (0,)), ((), ()))


def _lambda_init(layer_idx):
    return 0.8 - 0.6 * math.exp(-0.3 * layer_idx)


def _params(*sem):
    return pltpu.CompilerParams(dimension_semantics=sem, vmem_limit_bytes=VMEM_LIMIT)


def _sigmoid(x):
    return 1.0 / (1.0 + jnp.exp(-x))


def _silu(x):
    return x * _sigmoid(x)


def _mod_kernel(c_ref, w_ref, b_ref, o_ref):
    a = _silu(c_ref[...]).astype(BF16)
    o_ref[...] = jnp.dot(a, w_ref[...].astype(BF16), preferred_element_type=F32) + b_ref[...]


def _modulation(c_all, w_ada, b_ada, bn=512):
    rows, d = c_all.shape
    n = w_ada.shape[1]
    return pl.pallas_call(
        _mod_kernel,
        out_shape=jax.ShapeDtypeStruct((rows, n), F32),
        grid=(n // bn,),
        in_specs=[pl.BlockSpec((rows, d), lambda j: (0, 0)),
                  pl.BlockSpec((d, bn), lambda j: (0, j)),
                  pl.BlockSpec((1, bn), lambda j: (0, j))],
        out_specs=pl.BlockSpec((rows, bn), lambda j: (0, j)),
        compiler_params=_params("arbitrary"),
        name="adaln_mod",
    )(c_all, w_ada, b_ada)


def _proj_kernel(x_ref, shift_ref, scale_ref, g_ref, w_ref,
                 u_ref, gc_ref, q_ref, k_ref, v_ref, ga_ref, *, dc, da, q_scale):
    x = x_ref[...]
    y = x * lax.rsqrt(jnp.mean(x * x, axis=-1, keepdims=True) + EPS) * g_ref[...]
    h = (y * (1.0 + scale_ref[...]) + shift_ref[...]).astype(BF16)

    def seg(start, width):
        return jnp.dot(h, w_ref[:, start:start + width], preferred_element_type=F32)

    a = seg(0, dc)
    b = seg(dc, dc)
    u_ref[...] = a * _sigmoid(b)
    gc_ref[...] = _silu(seg(2 * dc, dc)).astype(gc_ref.dtype)
    q_ref[...] = (seg(3 * dc, da) * q_scale).astype(q_ref.dtype)
    k_ref[...] = seg(3 * dc + da, da)
    v_ref[...] = seg(3 * dc + 2 * da, da)
    ga_ref[...] = _silu(seg(3 * dc + 3 * da, da)).astype(ga_ref.dtype)


def _in_proj(x, shift, scale, g_pre, w_in_bf16, dc, da, bm, act_dtype):
    m, d = x.shape
    n = w_in_bf16.shape[1]
    per_row = shift.shape[0] != 1
    mod_spec = pl.BlockSpec((bm, d), lambda i: (i, 0)) if per_row else pl.BlockSpec((1, d), lambda i: (0, 0))
    dk = da // N_HEADS // 2
    kern = functools.partial(_proj_kernel, dc=dc, da=da, q_scale=dk ** -0.5)
    row = lambda w: pl.BlockSpec((bm, w), lambda i: (i, 0))
    return pl.pallas_call(
        kern,
        out_shape=(jax.ShapeDtypeStruct((m, dc), F32),
                   jax.ShapeDtypeStruct((m, dc), act_dtype),
                   jax.ShapeDtypeStruct((m, da), act_dtype),
                   jax.ShapeDtypeStruct((m, da), F32),
                   jax.ShapeDtypeStruct((m, da), F32),
                   jax.ShapeDtypeStruct((m, da), act_dtype)),
        grid=(m // bm,),
        in_specs=[row(d), mod_spec, mod_spec,
                  pl.BlockSpec((1, d), lambda i: (0, 0)),
                  pl.BlockSpec((d, n), lambda i: (0, 0), pipeline_mode=pl.Buffered(1))],
        out_specs=(row(dc), row(dc), row(da), row(da), row(da), row(da)),
        compiler_params=_params("arbitrary"),
        name="in_proj",
    )(x, shift, scale, g_pre, w_in_bf16)


def _ln_swish_gate(conv, g_ref, b_ref, gate):
    mu = jnp.mean(conv, axis=-1, keepdims=True)
    d = conv - mu
    var = jnp.mean(d * d, axis=-1, keepdims=True)
    y = d * lax.rsqrt(var + EPS) * g_ref[...] + b_ref[...]
    return _silu(y) * gate


def _conv_prompt_kernel(prev_ref, cur_ref, gc_ref, w_ref, cb_ref, g_ref, b_ref, o_ref,
                        ext_ref, conv_ref, *, bt, halo, rows):
    i = pl.program_id(0)
    prev = prev_ref[...]
    ext_ref[0:halo, :] = jnp.where(i == 0, jnp.zeros_like(prev), prev)
    ext_ref[halo:halo + bt, :] = cur_ref[...]
    off = halo - CONV_STATE
    for r in range(bt // rows):
        acc = jnp.broadcast_to(cb_ref[...], (rows, cb_ref.shape[1]))
        for j in range(CONV_WIDTH):
            acc = acc + w_ref[j:j + 1, :] * ext_ref[r * rows + off + j:r * rows + off + j + rows, :]
        conv_ref[r * rows:(r + 1) * rows, :] = acc
    o_ref[...] = _ln_swish_gate(conv_ref[...], g_ref, b_ref, gc_ref[...].astype(F32)).astype(o_ref.dtype)


def _conv_prompt(u, gc, conv_w, conv_b, g_cn, b_cn, bt=128, halo=32, rows=32):
    t, dc = u.shape
    kern = functools.partial(_conv_prompt_kernel, bt=bt, halo=halo, rows=rows)
    ratio = bt // halo
    vec = pl.BlockSpec((1, dc), lambda i: (0, 0))
    return pl.pallas_call(
        kern,
        out_shape=jax.ShapeDtypeStruct((t, dc), BF16),
        grid=(t // bt,),
        in_specs=[pl.BlockSpec((halo, dc), lambda i: (jnp.maximum(i * ratio - 1, 0), 0)),
                  pl.BlockSpec((bt, dc), lambda i: (i, 0)),
                  pl.BlockSpec((bt, dc), lambda i: (i, 0)),
                  pl.BlockSpec((CONV_WIDTH, dc), lambda i: (0, 0)),
                  vec, vec, vec],
        out_specs=pl.BlockSpec((bt, dc), lambda i: (i, 0)),
        scratch_shapes=[pltpu.VMEM((bt + halo, dc), F32), pltpu.VMEM((bt, dc), F32)],
        compiler_params=_params("arbitrary"),
        name="conv_prompt",
    )(u, u, gc, conv_w, conv_b, g_cn, b_cn)


def _conv_sample_kernel(state_ref, u_ref, gc_ref, w_ref, cb_ref, g_ref, b_ref,
                        o_ref, ns_ref, conv_ref):
    nb = state_ref.shape[0]
    w_hist = w_ref[0:CONV_STATE, :]

    for b in range(nb):
        st = state_ref[b]
        conv_ref[b:b + 1, :] = jnp.sum(st * w_hist, axis=0, keepdims=True)
        ns_ref[b, 0:CONV_STATE - 1, :] = state_ref[b, 1:CONV_STATE, :]
        ns_ref[b, CONV_STATE - 1:CONV_STATE, :] = u_ref[b:b + 1, :]
    conv = conv_ref[...] + w_ref[CONV_STATE:CONV_WIDTH, :] * u_ref[...] + cb_ref[...]
    o_ref[...] = _ln_swish_gate(conv, g_ref, b_ref, gc_ref[...].astype(F32)).astype(o_ref.dtype)


def _conv_sample(state, u, gc, conv_w, conv_b, g_cn, b_cn):
    nb, _, dc = state.shape
    return pl.pallas_call(
        _conv_sample_kernel,
        out_shape=(jax.ShapeDtypeStruct((nb, dc), BF16),
                   jax.ShapeDtypeStruct((nb, CONV_STATE, dc), F32)),
        scratch_shapes=[pltpu.VMEM((nb, dc), F32)],
        compiler_params=pltpu.CompilerParams(vmem_limit_bytes=VMEM_LIMIT),
        name="conv_sample",
    )(state, u, gc, conv_w, conv_b, g_cn, b_cn)


def _lambda_full(lq1_ref, lk1_ref, lq2_ref, lk2_ref, layer_idx):
    s1 = jnp.sum(lq1_ref[...] * lk1_ref[...], axis=-1, keepdims=True)
    s2 = jnp.sum(lq2_ref[...] * lk2_ref[...], axis=-1, keepdims=True)
    return jnp.exp(s1) - jnp.exp(s2) + _lambda_init(layer_idx)


def _subln_gate(o, g_ref, gate, layer_idx):
    y = o * lax.rsqrt(jnp.mean(o * o, axis=-1, keepdims=True) + EPS) * g_ref[...]
    return y * (1.0 - _lambda_init(layer_idx)) * gate


def _attn_prompt_kernel(q_ref, k_ref, v_ref, ga_ref, g_ref, lq1_ref, lk1_ref, lq2_ref, lk2_ref,
                        o_ref, kb_ref, vb_ref, qs_ref, m_ref, l_ref, acc_ref, *, bq, layer_idx):
    i = pl.program_id(1)

    @pl.when(i == 0)
    def _():
        kb_ref[...] = k_ref[...].astype(BF16)
        vb_ref[...] = v_ref[...].astype(BF16)

    q = q_ref[...]
    lane = lax.broadcasted_iota(jnp.int32, q.shape, 1)
    half = q.shape[1] // 2
    zero = jnp.zeros_like(q)
    qs_ref[0:bq, :] = jnp.where(lane < half, q, zero)
    qs_ref[bq:2 * bq, :] = jnp.where(lane >= half, q, zero)
    m_ref[...] = jnp.full(m_ref.shape, -jnp.inf, F32)
    l_ref[...] = jnp.zeros(l_ref.shape, F32)
    acc_ref[...] = jnp.zeros(acc_ref.shape, F32)

    def chunk(kstart, diagonal):
        kt = kb_ref[pl.ds(kstart, bq), :]
        vt = vb_ref[pl.ds(kstart, bq), :]
        s = lax.dot_general(kt, qs_ref[...], NT_DIMS, preferred_element_type=F32)
        if diagonal:
            kidx = lax.broadcasted_iota(jnp.int32, s.shape, 0)
            qidx = lax.broadcasted_iota(jnp.int32, s.shape, 1) & (bq - 1)
            s = jnp.where(kidx <= qidx, s, -jnp.inf)
        m_old = m_ref[...]
        m_new = jnp.maximum(m_old, jnp.max(s, axis=0, keepdims=True))
        p = jnp.exp(s - m_new)
        alpha = jnp.exp(m_old - m_new)
        l_ref[...] = alpha * l_ref[...] + jnp.sum(p, axis=0, keepdims=True)
        pv = lax.dot_general(vt, p.astype(BF16), TN_DIMS, preferred_element_type=F32)
        acc_ref[...] = alpha * acc_ref[...] + pv
        m_ref[...] = m_new

    chunk(pl.multiple_of(i * bq, bq), True)

    def body(j, carry):
        chunk(pl.multiple_of(j * bq, bq), False)
        return carry

    lax.fori_loop(0, i, body, 0)

    lam = _lambda_full(lq1_ref, lk1_ref, lq2_ref, lk2_ref, layer_idx)
    acc = acc_ref[...]
    inv_l = 1.0 / l_ref[...]
    o_t = acc[:, 0:bq] * inv_l[:, 0:bq] - lam * (acc[:, bq:2 * bq] * inv_l[:, bq:2 * bq])
    o = o_t.T
    o_ref[...] = _subln_gate(o, g_ref, ga_ref[...].astype(F32), layer_idx).astype(o_ref.dtype)


def _attn_prompt(q, k, v, ga, g_subln, lq1, lk1, lq2, lk2, layer_idx, bq=512):
    s, da = q.shape
    hd = da // N_HEADS
    kern = functools.partial(_attn_prompt_kernel, bq=bq, layer_idx=layer_idx)
    tile = pl.BlockSpec((bq, hd), lambda h, i: (i, h))
    full = pl.BlockSpec((s, hd), lambda h, i: (0, h))
    small = lambda a: pl.BlockSpec(a.shape, lambda h, i: (0, 0))
    return pl.pallas_call(
        kern,
        out_shape=jax.ShapeDtypeStruct((s, da), BF16),
        grid=(N_HEADS, s // bq),
        in_specs=[tile, full, full, tile, small(g_subln), small(lq1), small(lk1), small(lq2), small(lk2)],
        out_specs=tile,
        scratch_shapes=[pltpu.VMEM((s, hd), BF16), pltpu.VMEM((s, hd), BF16),
                        pltpu.VMEM((2 * bq, hd), BF16),
                        pltpu.VMEM((1, 2 * bq), F32), pltpu.VMEM((1, 2 * bq), F32),
                        pltpu.VMEM((hd, 2 * bq), F32)],
        compiler_params=_params("arbitrary", "arbitrary"),
        name="attn_prompt",
    )(q, k, v, ga, g_subln, lq1, lk1, lq2, lk2)


def _attn_decode_kernel(pt_ref, q_ref, kn_ref, vn_ref, ga_ref, g_ref, lq1_ref, lk1_ref, lq2_ref, lk2_ref,
                        *rest, pages, layer_idx):
    k_refs = rest[:pages]
    v_refs = rest[pages:2 * pages]
    o_ref, m_ref, l_ref, acc_ref = rest[2 * pages:]
    c = pl.program_id(1)
    nh = q_ref.shape[1]

    @pl.when(c == 0)
    def _():
        m_ref[...] = jnp.full(m_ref.shape, -jnp.inf, F32)
        l_ref[...] = jnp.zeros(l_ref.shape, F32)
        acc_ref[...] = jnp.zeros(acc_ref.shape, F32)

    q = q_ref[0].astype(F32)
    lane = lax.broadcasted_iota(jnp.int32, q.shape, 1)
    half = q.shape[1] // 2
    qx = jnp.concatenate([jnp.where(lane < half, q, 0.0), jnp.where(lane >= half, q, 0.0)], axis=0)
    qx_b = qx.astype(BF16)

    rows = k_refs[0].shape[1]
    col_head = lax.broadcasted_iota(jnp.int32, (2 * nh, rows), 1) & (nh - 1)
    row_head = lax.broadcasted_iota(jnp.int32, (2 * nh, rows), 0) & (nh - 1)
    own = col_head == row_head

    scores = []
    m_new = m_ref[...]
    for t in range(pages):
        s = lax.dot_general(qx_b, k_refs[t][0].astype(BF16), NT_DIMS, preferred_element_type=F32)
        s = jnp.where(own, s, -jnp.inf)
        scores.append(s)
        m_new = jnp.maximum(m_new, jnp.max(s, axis=-1, keepdims=True))
    alpha = jnp.exp(m_ref[...] - m_new)
    l_new = alpha * l_ref[...]
    acc = alpha * acc_ref[...]
    for t in range(pages):
        p = jnp.exp(scores[t] - m_new)
        l_new = l_new + jnp.sum(p, axis=-1, keepdims=True)
        acc = acc + jnp.dot(p.astype(BF16), v_refs[t][0].astype(BF16), preferred_element_type=F32)
    m_ref[...] = m_new
    l_ref[...] = l_new
    acc_ref[...] = acc

    @pl.when(c == pl.num_programs(1) - 1)
    def _():
        kn = kn_ref[0].astype(BF16).astype(F32)
        vn = vn_ref[0].astype(BF16).astype(F32)
        kn2 = jnp.concatenate([kn, kn], axis=0)
        vn2 = jnp.concatenate([vn, vn], axis=0)
        s_new = jnp.sum(qx_b.astype(F32) * kn2, axis=-1, keepdims=True)
        m_fin = jnp.maximum(m_new, s_new)
        a_fin = jnp.exp(m_new - m_fin)
        p_new = jnp.exp(s_new - m_fin)
        l_fin = a_fin * l_new + p_new
        acc_fin = a_fin * acc + p_new * vn2
        lam = _lambda_full(lq1_ref, lk1_ref, lq2_ref, lk2_ref, layer_idx)
        on = acc_fin / l_fin
        o = on[0:nh] - lam * on[nh:2 * nh]
        o_ref[0] = _subln_gate(o, g_ref, ga_ref[0].astype(F32), layer_idx).astype(o_ref.dtype)


def _attn_decode(page_table, q, k_new, v_new, ga, cache_k, cache_v, g_subln, lq1, lk1, lq2, lk2,
                 layer_idx, pages=8):
    nb, n_pages = page_table.shape
    n_pool, page, nh, hd = cache_k.shape
    ck = cache_k.reshape(n_pool, page * nh, hd)
    cv = cache_v.reshape(n_pool, page * nh, hd)
    pt = page_table.reshape(nb * n_pages)
    kern = functools.partial(_attn_decode_kernel, pages=pages, layer_idx=layer_idx)
    seq = pl.BlockSpec((1, nh, hd), lambda b, c, pt: (b, 0, 0))
    small = lambda a: pl.BlockSpec(a.shape, lambda b, c, pt: (0, 0))

    def page_spec(t):
        return pl.BlockSpec((1, page * nh, hd),
                            lambda b, c, pt: (pt[b * n_pages + c * pages + t], 0, 0))

    grid_spec = pltpu.PrefetchScalarGridSpec(
        num_scalar_prefetch=1,
        grid=(nb, n_pages // pages),
        in_specs=[seq, seq, seq, seq, small(g_subln), small(lq1), small(lk1), small(lq2), small(lk2)]
                 + [page_spec(t) for t in range(pages)] * 2,
        out_specs=seq,
        scratch_shapes=[pltpu.VMEM((2 * nh, 1), F32), pltpu.VMEM((2 * nh, 1), F32),
                        pltpu.VMEM((2 * nh, hd), F32)],
    )
    return pl.pallas_call(
        kern,
        out_shape=jax.ShapeDtypeStruct((nb, nh, hd), F32),
        grid_spec=grid_spec,
        compiler_params=_params("arbitrary", "arbitrary"),
        name="attn_decode",
    )(pt, q.reshape(nb, nh, hd), k_new.reshape(nb, nh, hd), v_new.reshape(nb, nh, hd),
      ga.reshape(nb, nh, hd), g_subln, lq1, lk1, lq2, lk2, *([ck] * pages), *([cv] * pages))


def _out_kernel(oc_ref, oa_ref, wc_ref, wa_ref, x_ref, gate_ref, g_ref, o_ref):
    y = jnp.dot(oc_ref[...].astype(BF16), wc_ref[...], preferred_element_type=F32)
    y = y + jnp.dot(oa_ref[...].astype(BF16), wa_ref[...], preferred_element_type=F32)
    yn = y * lax.rsqrt(jnp.mean(y * y, axis=-1, keepdims=True) + EPS) * g_ref[...]
    o_ref[...] = x_ref[...] + gate_ref[...] * yn


def _out_proj(oc, oa, w_out_bf16, x, gate, g_post, bm):
    m, d = x.shape
    dc = oc.shape[1]
    da = oa.shape[1]
    per_row = gate.shape[0] != 1
    assert dc % da == 0
    gate_spec = pl.BlockSpec((bm, d), lambda i: (i, 0)) if per_row else pl.BlockSpec((1, d), lambda i: (0, 0))
    return pl.pallas_call(
        _out_kernel,
        out_shape=jax.ShapeDtypeStruct((m, d), F32),
        grid=(m // bm,),
        in_specs=[pl.BlockSpec((bm, dc), lambda i: (i, 0)),
                  pl.BlockSpec((bm, da), lambda i: (i, 0)),
                  pl.BlockSpec((dc, d), lambda i: (0, 0)),
                  pl.BlockSpec((da, d), lambda i: (dc // da, 0)),
                  pl.BlockSpec((bm, d), lambda i: (i, 0)),
                  gate_spec,
                  pl.BlockSpec((1, d), lambda i: (0, 0))],
        out_specs=pl.BlockSpec((bm, d), lambda i: (i, 0)),
        compiler_params=_params("arbitrary"),
        name="out_proj",
    )(oc, oa, w_out_bf16, w_out_bf16, x, gate, g_post)


def kernel(x_prompt, x_sample, cache_k, cache_v, state_conv, page_table, c_prompt, c_sample,
           w_ada, b_ada, g_pre, g_post, w_in, conv_w, conv_b, g_cn, b_cn,
           lq1, lk1, lq2, lk2, g_subln, w_out):
    depth = w_ada.shape[0]
    bp, seq, d = x_prompt.shape
    nb = x_sample.shape[0]
    assert bp == 1 and x_sample.shape[1] == 1
    dc = conv_w.shape[2]
    da = w_out.shape[1] - dc
    hd = da // N_HEADS

    xp = x_prompt.reshape(seq, d)
    xs = x_sample.reshape(nb, d)
    pad = (-(bp + nb)) % 8
    c_all = jnp.concatenate([c_prompt, c_sample, jnp.zeros((pad, d), F32)], axis=0)
    row = lambda a: a.reshape(1, -1)

    kp_l, vp_l, cp_l, ks_l, vs_l, cs_l = [], [], [], [], [], []
    for l in range(depth):
        w_in_b = w_in[l].astype(BF16)
        w_out_b = w_out[l].astype(BF16)
        mod = _modulation(c_all, w_ada[l], row(b_ada[l]))
        shift, scale, gate = mod[:, 0:d], mod[:, d:2 * d], mod[:, 2 * d:3 * d]
        vecs = (row(conv_b[l]), row(g_cn[l]), row(b_cn[l]))
        lams = (row(lq1[l]), row(lk1[l]), row(lq2[l]), row(lk2[l]))
        g_sub = row(g_subln[l])

        u, gc, q, k, v, ga = _in_proj(xp, shift[0:1], scale[0:1], row(g_pre[l]), w_in_b, dc, da, bm=256, act_dtype=BF16)
        oc = _conv_prompt(u, gc, conv_w[l], *vecs)
        oa = _attn_prompt(q, k, v, ga, g_sub, *lams, layer_idx=l)
        xp = _out_proj(oc, oa, w_out_b, xp, gate[0:1], row(g_post[l]), bm=256)
        kp_l.append(k.reshape(bp, seq, N_HEADS, hd))
        vp_l.append(v.reshape(bp, seq, N_HEADS, hd))
        cp_l.append(u[seq - CONV_STATE:].reshape(bp, CONV_STATE, dc))

        sl = slice(bp, bp + nb)
        us, gcs, qs, ksn, vsn, gas = _in_proj(xs, shift[sl], scale[sl], row(g_pre[l]), w_in_b, dc, da, bm=nb, act_dtype=F32)
        ocs, new_state = _conv_sample(state_conv[l], us, gcs, conv_w[l], *vecs)
        oas = _attn_decode(page_table, qs, ksn, vsn, gas, cache_k[l], cache_v[l], g_sub, *lams, layer_idx=l)
        xs = _out_proj(ocs, oas.reshape(nb, da), w_out_b, xs, gate[sl], row(g_post[l]), bm=nb)
        ks_l.append(ksn.reshape(nb, 1, N_HEADS, hd))
        vs_l.append(vsn.reshape(nb, 1, N_HEADS, hd))
        cs_l.append(new_state)

    return (xp.reshape(bp, seq, d), xs.reshape(nb, 1, d),
            jnp.stack(kp_l), jnp.stack(vp_l), jnp.stack(cp_l),
            jnp.stack(ks_l), jnp.stack(vs_l), jnp.stack(cs_l))
```

```python
import functools
import math

import jax
import jax.numpy as jnp
from jax import lax
from jax.experimental import pallas as pl
from jax.experimental.pallas import tpu as pltpu

F32 = jnp.float32
BF16 = jnp.bfloat16

N_HEADS = 8
CONV_WIDTH = 31
CONV_STATE = CONV_WIDTH - 1
PAGE_SIZE = 128
EPS = 1e-6

LANES = 128
SUBLANES = 8
VMEM_LIMIT = 56 * 1024 * 1024

NT_DIMS = (((1,), (1,)), ((), ()))
TN_DIMS = (((0,), (0,)), ((), ()))


def _lambda_init(layer_idx):
    return 0.8 - 0.6 * math.exp(-0.3 * layer_idx)


def _params(*sem):
    return pltpu.CompilerParams(dimension_semantics=sem, vmem_limit_bytes=VMEM_LIMIT)


def _sigmoid(x):
    return 1.0 / (1.0 + jnp.exp(-x))


def _silu(x):
    return x * _sigmoid(x)


def _mod_kernel(c_ref, w_ref, b_ref, o_ref):
    a = _silu(c_ref[...]).astype(BF16)
    o_ref[...] = jnp.dot(a, w_ref[...].astype(BF16), preferred_element_type=F32) + b_ref[...]


def _modulation(c_all, w_ada, b_ada, bn=512):
    rows, d = c_all.shape
    n = w_ada.shape[1]
    return pl.pallas_call(
        _mod_kernel,
        out_shape=jax.ShapeDtypeStruct((rows, n), F32),
        grid=(n // bn,),
        in_specs=[pl.BlockSpec((rows, d), lambda j: (0, 0)),
                  pl.BlockSpec((d, bn), lambda j: (0, j)),
                  pl.BlockSpec((1, bn), lambda j: (0, j))],
        out_specs=pl.BlockSpec((rows, bn), lambda j: (0, j)),
        compiler_params=_params("arbitrary"),
        name="adaln_mod",
    )(c_all, w_ada, b_ada)


def _proj_kernel(x_ref, shift_ref, scale_ref, g_ref, w_ref,
                 u_ref, gc_ref, q_ref, k_ref, v_ref, ga_ref, *, dc, da, q_scale):
    x = x_ref[...]
    y = x * lax.rsqrt(jnp.mean(x * x, axis=-1, keepdims=True) + EPS) * g_ref[...]
    h = (y * (1.0 + scale_ref[...]) + shift_ref[...]).astype(BF16)

    def seg(start, width):
        return jnp.dot(h, w_ref[:, start:start + width], preferred_element_type=F32)

    a = seg(0, dc)
    b = seg(dc, dc)
    u_ref[...] = a * _sigmoid(b)
    gc_ref[...] = _silu(seg(2 * dc, dc)).astype(gc_ref.dtype)
    q_ref[...] = (seg(3 * dc, da) * q_scale).astype(q_ref.dtype)
    k_ref[...] = seg(3 * dc + da, da)
    v_ref[...] = seg(3 * dc + 2 * da, da)
    ga_ref[...] = _silu(seg(3 * dc + 3 * da, da)).astype(ga_ref.dtype)


def _in_proj(x, shift, scale, g_pre, w_in_bf16, dc, da, bm, act_dtype, q_scale):
    m, d = x.shape
    n = w_in_bf16.shape[1]
    per_row = shift.shape[0] != 1
    mod_spec = pl.BlockSpec((bm, d), lambda i: (i, 0)) if per_row else pl.BlockSpec((1, d), lambda i: (0, 0))
    kern = functools.partial(_proj_kernel, dc=dc, da=da, q_scale=q_scale)
    row = lambda w: pl.BlockSpec((bm, w), lambda i: (i, 0))
    return pl.pallas_call(
        kern,
        out_shape=(jax.ShapeDtypeStruct((m, dc), F32),
                   jax.ShapeDtypeStruct((m, dc), act_dtype),
                   jax.ShapeDtypeStruct((m, da), act_dtype),
                   jax.ShapeDtypeStruct((m, da), F32),
                   jax.ShapeDtypeStruct((m, da), F32),
                   jax.ShapeDtypeStruct((m, da), act_dtype)),
        grid=(m // bm,),
        in_specs=[row(d), mod_spec, mod_spec,
                  pl.BlockSpec((1, d), lambda i: (0, 0)),
                  pl.BlockSpec((d, n), lambda i: (0, 0), pipeline_mode=pl.Buffered(1))],
        out_specs=(row(dc), row(dc), row(da), row(da), row(da), row(da)),
        compiler_params=_params("arbitrary"),
        name="in_proj",
    )(x, shift, scale, g_pre, w_in_bf16)


def _ln_swish_gate(conv, g_ref, b_ref, gate):
    mu = jnp.mean(conv, axis=-1, keepdims=True)
    d = conv - mu
    var = jnp.mean(d * d, axis=-1, keepdims=True)
    y = d * lax.rsqrt(var + EPS) * g_ref[...] + b_ref[...]
    return _silu(y) * gate


def _conv_prompt_kernel(prev_ref, cur_ref, gc_ref, w_ref, cb_ref, g_ref, b_ref, o_ref,
                        ext_ref, sh_ref, conv_ref, *, bt, halo, rows):
    i = pl.program_id(0)
    prev = prev_ref[...]
    ext_ref[0:halo, :] = jnp.where(i == 0, jnp.zeros_like(prev), prev)
    ext_ref[halo:halo + bt, :] = cur_ref[...]
    span = sh_ref.shape[1]
    for b in range(1, SUBLANES):
        sh_ref[b - 1] = ext_ref[b:b + span, :]
    off = halo - CONV_STATE
    reps = rows // SUBLANES
    for c in range(cur_ref.shape[1] // LANES):
        cs = slice(c * LANES, (c + 1) * LANES)
        taps = [jnp.concatenate([w_ref[j, :, cs]] * reps, axis=0) for j in range(CONV_WIDTH)]
        bias = jnp.broadcast_to(cb_ref[:, cs], (rows, LANES))
        for r in range(bt // rows):
            acc = bias
            for j in range(CONV_WIDTH):
                a, b = divmod(off + j, SUBLANES)
                base = r * rows + SUBLANES * a
                src = ext_ref[base:base + rows, cs] if b == 0 else sh_ref[b - 1, base:base + rows, cs]
                acc = acc + taps[j] * src
            conv_ref[r * rows:(r + 1) * rows, cs] = acc
    o_ref[...] = _ln_swish_gate(conv_ref[...], g_ref, b_ref, gc_ref[...].astype(F32)).astype(o_ref.dtype)


def _conv_prompt(u, gc, conv_w8, conv_b, g_cn, b_cn, bt=128, halo=32, rows=32):
    t, dc = u.shape
    kern = functools.partial(_conv_prompt_kernel, bt=bt, halo=halo, rows=rows)
    ratio = bt // halo
    vec = pl.BlockSpec((1, dc), lambda i: (0, 0))
    return pl.pallas_call(
        kern,
        out_shape=jax.ShapeDtypeStruct((t, dc), BF16),
        grid=(t // bt,),
        in_specs=[pl.BlockSpec((halo, dc), lambda i: (jnp.maximum(i * ratio - 1, 0), 0)),
                  pl.BlockSpec((bt, dc), lambda i: (i, 0)),
                  pl.BlockSpec((bt, dc), lambda i: (i, 0)),
                  pl.BlockSpec((CONV_WIDTH, SUBLANES, dc), lambda i: (0, 0, 0)),
                  vec, vec, vec],
        out_specs=pl.BlockSpec((bt, dc), lambda i: (i, 0)),
        scratch_shapes=[pltpu.VMEM((bt + halo, dc), F32),
                        pltpu.VMEM((SUBLANES - 1, bt + halo - SUBLANES, dc), F32),
                        pltpu.VMEM((bt, dc), F32)],
        compiler_params=_params("arbitrary"),
        name="conv_prompt",
    )(u, u, gc, conv_w8, conv_b, g_cn, b_cn)


def _conv_sample_kernel(state_ref, u_ref, gc_ref, w_ref, cb_ref, g_ref, b_ref,
                        o_ref, ns_ref, conv_ref):
    nb = state_ref.shape[0]
    w_hist = w_ref[0:CONV_STATE, :]

    for b in range(nb):
        st = state_ref[b]
        conv_ref[b:b + 1, :] = jnp.sum(st * w_hist, axis=0, keepdims=True)
        ns_ref[b, 0:CONV_STATE - 1, :] = state_ref[b, 1:CONV_STATE, :]
        ns_ref[b, CONV_STATE - 1:CONV_STATE, :] = u_ref[b:b + 1, :]
    conv = conv_ref[...] + w_ref[CONV_STATE:CONV_WIDTH, :] * u_ref[...] + cb_ref[...]
    o_ref[...] = _ln_swish_gate(conv, g_ref, b_ref, gc_ref[...].astype(F32)).astype(o_ref.dtype)


def _conv_sample(state, u, gc, conv_w, conv_b, g_cn, b_cn):
    nb, _, dc = state.shape
    return pl.pallas_call(
        _conv_sample_kernel,
        out_shape=(jax.ShapeDtypeStruct((nb, dc), BF16),
                   jax.ShapeDtypeStruct((nb, CONV_STATE, dc), F32)),
        scratch_shapes=[pltpu.VMEM((nb, dc), F32)],
        compiler_params=pltpu.CompilerParams(vmem_limit_bytes=VMEM_LIMIT),
        name="conv_sample",
    )(state, u, gc, conv_w, conv_b, g_cn, b_cn)


def _lambda_full(lq1_ref, lk1_ref, lq2_ref, lk2_ref, layer_idx):
    s1 = jnp.sum(lq1_ref[...] * lk1_ref[...], axis=-1, keepdims=True)
    s2 = jnp.sum(lq2_ref[...] * lk2_ref[...], axis=-1, keepdims=True)
    return jnp.exp(s1) - jnp.exp(s2) + _lambda_init(layer_idx)


def _subln_gate(o, g_ref, gate, layer_idx):
    y = o * lax.rsqrt(jnp.mean(o * o, axis=-1, keepdims=True) + EPS) * g_ref[...]
    return y * (1.0 - _lambda_init(layer_idx)) * gate


def _attn_prompt_kernel(q_ref, k_ref, v_ref, ga_ref, g_ref, lq1_ref, lk1_ref, lq2_ref, lk2_ref,
                        o_ref, kb_ref, vt_ref, qs_ref, sa_ref, sb_ref, mxa_ref, mxb_ref,
                        m_ref, l_ref, acc_ref, *, bq, layer_idx):
    i = pl.program_id(1)
    n_chunks = kb_ref.shape[0]

    @pl.when(i == 0)
    def _():
        for c in range(n_chunks):
            kb_ref[c] = k_ref[c * bq:(c + 1) * bq, :].astype(BF16)
            vt_ref[c] = v_ref[c * bq:(c + 1) * bq, :].T.astype(BF16)

    q = q_ref[...]
    lane = lax.broadcasted_iota(jnp.int32, q.shape, 1)
    half = q.shape[1] // 2
    zero = jnp.zeros_like(q)
    qs_ref[0:bq, :] = jnp.where(lane < half, q, zero)
    qs_ref[bq:2 * bq, :] = jnp.where(lane >= half, q, zero)
    m_ref[...] = jnp.full(m_ref.shape, -jnp.inf, F32)
    l_ref[...] = jnp.zeros(l_ref.shape, F32)
    acc_ref[...] = jnp.zeros(acc_ref.shape, F32)

    def scores(c, s_ref, mx_ref, diagonal=False):
        s = lax.dot_general(kb_ref[c], qs_ref[...], NT_DIMS, preferred_element_type=F32)
        if diagonal:
            kidx = lax.broadcasted_iota(jnp.int32, s.shape, 0)
            qidx = lax.broadcasted_iota(jnp.int32, s.shape, 1) & (bq - 1)
            s = jnp.where(kidx <= qidx, s, -jnp.inf)
        s_ref[...] = s
        mx_ref[...] = jnp.max(s, axis=0, keepdims=True)

    def accumulate(c, s_ref, mx_ref):
        m_old = m_ref[...]
        m_new = jnp.maximum(m_old, mx_ref[...])
        p = jnp.exp2(s_ref[...] - m_new)
        alpha = jnp.exp2(m_old - m_new)
        l_ref[...] = alpha * l_ref[...] + jnp.sum(p, axis=0, keepdims=True)
        pv = jnp.dot(vt_ref[c], p.astype(BF16), preferred_element_type=F32)
        acc_ref[...] = alpha * acc_ref[...] + pv
        m_ref[...] = m_new

    scores(i, sa_ref, mxa_ref, diagonal=True)

    def body(p, carry):
        scores(2 * p, sb_ref, mxb_ref)
        accumulate(jnp.where(p == 0, i, 2 * p - 1), sa_ref, mxa_ref)
        scores(jnp.minimum(2 * p + 1, i - 1), sa_ref, mxa_ref)
        accumulate(2 * p, sb_ref, mxb_ref)
        return carry

    lax.fori_loop(0, (i + 1) // 2, body, 0)

    @pl.when(i % 2 == 0)
    def _():
        accumulate(jnp.maximum(i - 1, 0), sa_ref, mxa_ref)

    lam = _lambda_full(lq1_ref, lk1_ref, lq2_ref, lk2_ref, layer_idx)
    acc = acc_ref[...]
    inv_l = 1.0 / l_ref[...]
    o_t = acc[:, 0:bq] * inv_l[:, 0:bq] - lam * (acc[:, bq:2 * bq] * inv_l[:, bq:2 * bq])
    o = o_t.T
    o_ref[...] = _subln_gate(o, g_ref, ga_ref[...].astype(F32), layer_idx).astype(o_ref.dtype)


def _attn_prompt(q, k, v, ga, g_subln, lq1, lk1, lq2, lk2, layer_idx, bq=512):
    s, da = q.shape
    hd = da // N_HEADS
    kern = functools.partial(_attn_prompt_kernel, bq=bq, layer_idx=layer_idx)
    tile = pl.BlockSpec((bq, hd), lambda h, i: (i, h))
    full = pl.BlockSpec((s, hd), lambda h, i: (0, h))
    small = lambda a: pl.BlockSpec(a.shape, lambda h, i: (0, 0))
    return pl.pallas_call(
        kern,
        out_shape=jax.ShapeDtypeStruct((s, da), BF16),
        grid=(N_HEADS, s // bq),
        in_specs=[tile, full, full, tile, small(g_subln), small(lq1), small(lk1), small(lq2), small(lk2)],
        out_specs=tile,
        scratch_shapes=[pltpu.VMEM((s // bq, bq, hd), BF16),
                        pltpu.VMEM((s // bq, hd, bq), BF16),
                        pltpu.VMEM((2 * bq, hd), BF16),
                        pltpu.VMEM((bq, 2 * bq), F32), pltpu.VMEM((bq, 2 * bq), F32),
                        pltpu.VMEM((1, 2 * bq), F32), pltpu.VMEM((1, 2 * bq), F32),
                        pltpu.VMEM((1, 2 * bq), F32), pltpu.VMEM((1, 2 * bq), F32),
                        pltpu.VMEM((hd, 2 * bq), F32)],
        compiler_params=_params("arbitrary", "arbitrary"),
        name="attn_prompt",
    )(q, k, v, ga, g_subln, lq1, lk1, lq2, lk2)


def _attn_decode_kernel(pt_ref, q_ref, kn_ref, vn_ref, ga_ref, g_ref, lq1_ref, lk1_ref, lq2_ref, lk2_ref,
                        *rest, pages, layer_idx):
    k_refs = rest[:pages]
    v_refs = rest[pages:2 * pages]
    o_ref, m_ref, l_ref, acc_ref = rest[2 * pages:]
    c = pl.program_id(1)
    nh = q_ref.shape[1]

    @pl.when(c == 0)
    def _():
        m_ref[...] = jnp.full(m_ref.shape, -jnp.inf, F32)
        l_ref[...] = jnp.zeros(l_ref.shape, F32)
        acc_ref[...] = jnp.zeros(acc_ref.shape, F32)

    q = q_ref[0].astype(F32)
    lane = lax.broadcasted_iota(jnp.int32, q.shape, 1)
    half = q.shape[1] // 2
    qx = jnp.concatenate([jnp.where(lane < half, q, 0.0), jnp.where(lane >= half, q, 0.0)], axis=0)
    qx_b = qx.astype(BF16)

    rows = k_refs[0].shape[1]
    col_head = lax.broadcasted_iota(jnp.int32, (2 * nh, rows), 1) & (nh - 1)
    row_head = lax.broadcasted_iota(jnp.int32, (2 * nh, rows), 0) & (nh - 1)
    own = col_head == row_head

    scores = []
    m_new = m_ref[...]
    for t in range(pages):
        s = lax.dot_general(qx_b, k_refs[t][0].astype(BF16), NT_DIMS, preferred_element_type=F32)
        s = jnp.where(own, s, -jnp.inf)
        scores.append(s)
        m_new = jnp.maximum(m_new, jnp.max(s, axis=-1, keepdims=True))
    alpha = jnp.exp(m_ref[...] - m_new)
    l_new = alpha * l_ref[...]
    acc = alpha * acc_ref[...]
    for t in range(pages):
        p = jnp.exp(scores[t] - m_new)
        l_new = l_new + jnp.sum(p, axis=-1, keepdims=True)
        acc = acc + jnp.dot(p.astype(BF16), v_refs[t][0].astype(BF16), preferred_element_type=F32)
    m_ref[...] = m_new
    l_ref[...] = l_new
    acc_ref[...] = acc

    @pl.when(c == pl.num_programs(1) - 1)
    def _():
        kn = kn_ref[0].astype(BF16).astype(F32)
        vn = vn_ref[0].astype(BF16).astype(F32)
        kn2 = jnp.concatenate([kn, kn], axis=0)
        vn2 = jnp.concatenate([vn, vn], axis=0)
        s_new = jnp.sum(qx_b.astype(F32) * kn2, axis=-1, keepdims=True)
        m_fin = jnp.maximum(m_new, s_new)
        a_fin = jnp.exp(m_new - m_fin)
        p_new = jnp.exp(s_new - m_fin)
        l_fin = a_fin * l_new + p_new
        acc_fin = a_fin * acc + p_new * vn2
        lam = _lambda_full(lq1_ref, lk1_ref, lq2_ref, lk2_ref, layer_idx)
        on = acc_fin / l_fin
        o = on[0:nh] - lam * on[nh:2 * nh]
        o_ref[0] = _subln_gate(o, g_ref, ga_ref[0].astype(F32), layer_idx).astype(o_ref.dtype)


def _attn_decode(page_table, q, k_new, v_new, ga, cache_k, cache_v, g_subln, lq1, lk1, lq2, lk2,
                 layer_idx, pages=8):
    nb, n_pages = page_table.shape
    n_pool, page, nh, hd = cache_k.shape
    ck = cache_k.reshape(n_pool, page * nh, hd)
    cv = cache_v.reshape(n_pool, page * nh, hd)
    pt = page_table.reshape(nb * n_pages)
    kern = functools.partial(_attn_decode_kernel, pages=pages, layer_idx=layer_idx)
    seq = pl.BlockSpec((1, nh, hd), lambda b, c, pt: (b, 0, 0))
    small = lambda a: pl.BlockSpec(a.shape, lambda b, c, pt: (0, 0))

    def page_spec(t):
        return pl.BlockSpec((1, page * nh, hd),
                            lambda b, c, pt: (pt[b * n_pages + c * pages + t], 0, 0))

    grid_spec = pltpu.PrefetchScalarGridSpec(
        num_scalar_prefetch=1,
        grid=(nb, n_pages // pages),
        in_specs=[seq, seq, seq, seq, small(g_subln), small(lq1), small(lk1), small(lq2), small(lk2)]
                 + [page_spec(t) for t in range(pages)] * 2,
        out_specs=seq,
        scratch_shapes=[pltpu.VMEM((2 * nh, 1), F32), pltpu.VMEM((2 * nh, 1), F32),
                        pltpu.VMEM((2 * nh, hd), F32)],
    )
    return pl.pallas_call(
        kern,
        out_shape=jax.ShapeDtypeStruct((nb, nh, hd), F32),
        grid_spec=grid_spec,
        compiler_params=_params("arbitrary", "arbitrary"),
        name="attn_decode",
    )(pt, q.reshape(nb, nh, hd), k_new.reshape(nb, nh, hd), v_new.reshape(nb, nh, hd),
      ga.reshape(nb, nh, hd), g_subln, lq1, lk1, lq2, lk2, *([ck] * pages), *([cv] * pages))


def _out_kernel(oc_ref, oa_ref, wc_ref, wa_ref, x_ref, gate_ref, g_ref, o_ref):
    y = jnp.dot(oc_ref[...].astype(BF16), wc_ref[...], preferred_element_type=F32)
    y = y + jnp.dot(oa_ref[...].astype(BF16), wa_ref[...], preferred_element_type=F32)
    yn = y * lax.rsqrt(jnp.mean(y * y, axis=-1, keepdims=True) + EPS) * g_ref[...]
    o_ref[...] = x_ref[...] + gate_ref[...] * yn


def _out_proj(oc, oa, w_out_bf16, x, gate, g_post, bm):
    m, d = x.shape
    dc = oc.shape[1]
    da = oa.shape[1]
    per_row = gate.shape[0] != 1
    assert dc % da == 0
    gate_spec = pl.BlockSpec((bm, d), lambda i: (i, 0)) if per_row else pl.BlockSpec((1, d), lambda i: (0, 0))
    return pl.pallas_call(
        _out_kernel,
        out_shape=jax.ShapeDtypeStruct((m, d), F32),
        grid=(m // bm,),
        in_specs=[pl.BlockSpec((bm, dc), lambda i: (i, 0)),
                  pl.BlockSpec((bm, da), lambda i: (i, 0)),
                  pl.BlockSpec((dc, d), lambda i: (0, 0)),
                  pl.BlockSpec((da, d), lambda i: (dc // da, 0)),
                  pl.BlockSpec((bm, d), lambda i: (i, 0)),
                  gate_spec,
                  pl.BlockSpec((1, d), lambda i: (0, 0))],
        out_specs=pl.BlockSpec((bm, d), lambda i: (i, 0)),
        compiler_params=_params("arbitrary"),
        name="out_proj",
    )(oc, oa, w_out_bf16, w_out_bf16, x, gate, g_post)


def kernel(x_prompt, x_sample, cache_k, cache_v, state_conv, page_table, c_prompt, c_sample,
           w_ada, b_ada, g_pre, g_post, w_in, conv_w, conv_b, g_cn, b_cn,
           lq1, lk1, lq2, lk2, g_subln, w_out):
    depth = w_ada.shape[0]
    bp, seq, d = x_prompt.shape
    nb = x_sample.shape[0]
    assert bp == 1 and x_sample.shape[1] == 1
    dc = conv_w.shape[2]
    da = w_out.shape[1] - dc
    hd = da // N_HEADS
    dk = hd // 2

    xp = x_prompt.reshape(seq, d)
    xs = x_sample.reshape(nb, d)
    pad = (-(bp + nb)) % 8
    c_all = jnp.concatenate([c_prompt, c_sample, jnp.zeros((pad, d), F32)], axis=0)
    row = lambda a: a.reshape(1, -1)

    kp_l, vp_l, cp_l, ks_l, vs_l, cs_l = [], [], [], [], [], []
    for l in range(depth):
        w_in_b = w_in[l].astype(BF16)
        w_out_b = w_out[l].astype(BF16)
        mod = _modulation(c_all, w_ada[l], row(b_ada[l]))
        shift, scale, gate = mod[:, 0:d], mod[:, d:2 * d], mod[:, 2 * d:3 * d]
        vecs = (row(conv_b[l]), row(g_cn[l]), row(b_cn[l]))
        lams = (row(lq1[l]), row(lk1[l]), row(lq2[l]), row(lk2[l]))
        g_sub = row(g_subln[l])
        conv_w8 = jnp.broadcast_to(conv_w[l][:, None, :], (CONV_WIDTH, SUBLANES, dc))

        u, gc, q, k, v, ga = _in_proj(xp, shift[0:1], scale[0:1], row(g_pre[l]), w_in_b, dc, da, bm=256,
                                      act_dtype=BF16, q_scale=dk ** -0.5 * math.log2(math.e))
        oc = _conv_prompt(u, gc, conv_w8, *vecs)
        oa = _attn_prompt(q, k, v, ga, g_sub, *lams, layer_idx=l)
        xp = _out_proj(oc, oa, w_out_b, xp, gate[0:1], row(g_post[l]), bm=256)
        kp_l.append(k.reshape(bp, seq, N_HEADS, hd))
        vp_l.append(v.reshape(bp, seq, N_HEADS, hd))
        cp_l.append(u[seq - CONV_STATE:].reshape(bp, CONV_STATE, dc))

        sl = slice(bp, bp + nb)
        us, gcs, qs, ksn, vsn, gas = _in_proj(xs, shift[sl], scale[sl], row(g_pre[l]), w_in_b, dc, da, bm=nb,
                                              act_dtype=F32, q_scale=dk ** -0.5)
        ocs, new_state = _conv_sample(state_conv[l], us, gcs, conv_w[l], *vecs)
        oas = _attn_decode(page_table, qs, ksn, vsn, gas, cache_k[l], cache_v[l], g_sub, *lams, layer_idx=l)
        xs = _out_proj(ocs, oas.reshape(nb, da), w_out_b, xs, gate[sl], row(g_post[l]), bm=nb)
        ks_l.append(ksn.reshape(nb, 1, N_HEADS, hd))
        vs_l.append(vsn.reshape(nb, 1, N_HEADS, hd))
        cs_l.append(new_state)

    return (xp.reshape(bp, seq, d), xs.reshape(nb, 1, d),
            jnp.stack(kp_l), jnp.stack(vp_l), jnp.stack(cp_l),
            jnp.stack(ks_l), jnp.stack(vs_l), jnp.stack(cs_l))
```

```python
import functools
import math

import jax
import jax.numpy as jnp
from jax import lax
from jax.experimental import pallas as pl
from jax.experimental.pallas import tpu as pltpu

F32 = jnp.float32
BF16 = jnp.bfloat16

N_HEADS = 8
CONV_WIDTH = 31
CONV_STATE = CONV_WIDTH - 1
EPS = 1e-6

LANES = 128
SUBLANES = 8
ONES_ROWS = 16
VMEM_LIMIT = 56 * 1024 * 1024

NT_DIMS = (((1,), (1,)), ((), ()))


def _lambda_init(layer_idx):
    return 0.8 - 0.6 * math.exp(-0.3 * layer_idx)


def _params(*sem):
    return pltpu.CompilerParams(dimension_semantics=sem, vmem_limit_bytes=VMEM_LIMIT)


def _sigmoid(x):
    return 1.0 / (1.0 + jnp.exp(-x))


def _silu(x):
    return x * _sigmoid(x)


def _mod_kernel(c_ref, w_ref, b_ref, o_ref):
    a = _silu(c_ref[...]).astype(BF16)
    o_ref[...] = jnp.dot(a, w_ref[...].astype(BF16), preferred_element_type=F32) + b_ref[...]


def _modulation(c_all, w_ada, b_ada, bn=512):
    rows, d = c_all.shape
    n = w_ada.shape[1]
    return pl.pallas_call(
        _mod_kernel,
        out_shape=jax.ShapeDtypeStruct((rows, n), F32),
        grid=(n // bn,),
        in_specs=[pl.BlockSpec((rows, d), lambda j: (0, 0)),
                  pl.BlockSpec((d, bn), lambda j: (0, j)),
                  pl.BlockSpec((1, bn), lambda j: (0, j))],
        out_specs=pl.BlockSpec((rows, bn), lambda j: (0, j)),
        compiler_params=_params("arbitrary"),
        name="adaln_mod",
    )(c_all, w_ada, b_ada)


def _proj_kernel(x_ref, shift_ref, scale_ref, g_ref, w_ref,
                 u_ref, gc_ref, q_ref, k_ref, v_ref, ga_ref, *, dc, da, q_scale):
    x = x_ref[...]
    y = x * lax.rsqrt(jnp.mean(x * x, axis=-1, keepdims=True) + EPS) * g_ref[...]
    h = (y * (1.0 + scale_ref[...]) + shift_ref[...]).astype(BF16)

    def seg(start, width):
        return jnp.dot(h, w_ref[:, start:start + width], preferred_element_type=F32)

    a = seg(0, dc)
    b = seg(dc, dc)
    u_ref[...] = a * _sigmoid(b)
    gc_ref[...] = _silu(seg(2 * dc, dc)).astype(gc_ref.dtype)
    q_ref[...] = (seg(3 * dc, da) * q_scale).astype(q_ref.dtype)
    k_ref[...] = seg(3 * dc + da, da)
    v_ref[...] = seg(3 * dc + 2 * da, da)
    ga_ref[...] = _silu(seg(3 * dc + 3 * da, da)).astype(ga_ref.dtype)


def _in_proj(x, shift, scale, g_pre, w_in_bf16, dc, da, bm, act_dtype, q_scale):
    m, d = x.shape
    n = w_in_bf16.shape[1]
    per_row = shift.shape[0] != 1
    mod_spec = pl.BlockSpec((bm, d), lambda i: (i, 0)) if per_row else pl.BlockSpec((1, d), lambda i: (0, 0))
    kern = functools.partial(_proj_kernel, dc=dc, da=da, q_scale=q_scale)
    row = lambda w: pl.BlockSpec((bm, w), lambda i: (i, 0))
    return pl.pallas_call(
        kern,
        out_shape=(jax.ShapeDtypeStruct((m, dc), F32),
                   jax.ShapeDtypeStruct((m, dc), act_dtype),
                   jax.ShapeDtypeStruct((m, da), act_dtype),
                   jax.ShapeDtypeStruct((m, da), F32),
                   jax.ShapeDtypeStruct((m, da), F32),
                   jax.ShapeDtypeStruct((m, da), act_dtype)),
        grid=(m // bm,),
        in_specs=[row(d), mod_spec, mod_spec,
                  pl.BlockSpec((1, d), lambda i: (0, 0)),
                  pl.BlockSpec((d, n), lambda i: (0, 0), pipeline_mode=pl.Buffered(1))],
        out_specs=(row(dc), row(dc), row(da), row(da), row(da), row(da)),
        compiler_params=_params("arbitrary"),
        name="in_proj",
    )(x, shift, scale, g_pre, w_in_bf16)


def _ln_swish_gate(conv, g_ref, b_ref, gate):
    mu = jnp.mean(conv, axis=-1, keepdims=True)
    d = conv - mu
    var = jnp.mean(d * d, axis=-1, keepdims=True)
    y = d * lax.rsqrt(var + EPS) * g_ref[...] + b_ref[...]
    return _silu(y) * gate


def _conv_prompt_kernel(prev_ref, cur_ref, gc_ref, w_ref, cb_ref, g_ref, b_ref, o_ref,
                        ext_ref, sh_ref, conv_ref, *, bt, halo, rows):
    i = pl.program_id(0)
    prev = prev_ref[...]
    ext_ref[0:halo, :] = jnp.where(i == 0, jnp.zeros_like(prev), prev)
    ext_ref[halo:halo + bt, :] = cur_ref[...]
    span = sh_ref.shape[1]
    for b in range(1, SUBLANES):
        sh_ref[b - 1] = ext_ref[b:b + span, :]
    off = halo - CONV_STATE
    reps = rows // SUBLANES
    for c in range(cur_ref.shape[1] // LANES):
        cs = slice(c * LANES, (c + 1) * LANES)
        taps = [jnp.concatenate([w_ref[j, :, cs]] * reps, axis=0) for j in range(CONV_WIDTH)]
        bias = jnp.broadcast_to(cb_ref[:, cs], (rows, LANES))
        for r in range(bt // rows):
            acc = bias
            for j in range(CONV_WIDTH):
                a, b = divmod(off + j, SUBLANES)
                base = r * rows + SUBLANES * a
                src = ext_ref[base:base + rows, cs] if b == 0 else sh_ref[b - 1, base:base + rows, cs]
                acc = acc + taps[j] * src
            conv_ref[r * rows:(r + 1) * rows, cs] = acc
    o_ref[...] = _ln_swish_gate(conv_ref[...], g_ref, b_ref, gc_ref[...].astype(F32)).astype(o_ref.dtype)


def _conv_prompt(u, gc, conv_w8, conv_b, g_cn, b_cn, bt=128, halo=32, rows=32):
    t, dc = u.shape
    kern = functools.partial(_conv_prompt_kernel, bt=bt, halo=halo, rows=rows)
    ratio = bt // halo
    vec = pl.BlockSpec((1, dc), lambda i: (0, 0))
    return pl.pallas_call(
        kern,
        out_shape=jax.ShapeDtypeStruct((t, dc), BF16),
        grid=(t // bt,),
        in_specs=[pl.BlockSpec((halo, dc), lambda i: (jnp.maximum(i * ratio - 1, 0), 0)),
                  pl.BlockSpec((bt, dc), lambda i: (i, 0)),
                  pl.BlockSpec((bt, dc), lambda i: (i, 0)),
                  pl.BlockSpec((CONV_WIDTH, SUBLANES, dc), lambda i: (0, 0, 0)),
                  vec, vec, vec],
        out_specs=pl.BlockSpec((bt, dc), lambda i: (i, 0)),
        scratch_shapes=[pltpu.VMEM((bt + halo, dc), F32),
                        pltpu.VMEM((SUBLANES - 1, bt + halo - SUBLANES, dc), F32),
                        pltpu.VMEM((bt, dc), F32)],
        compiler_params=_params("arbitrary"),
        name="conv_prompt",
    )(u, u, gc, conv_w8, conv_b, g_cn, b_cn)


def _conv_sample_kernel(state_ref, u_ref, gc_ref, w_ref, cb_ref, g_ref, b_ref,
                        o_ref, ns_ref, conv_ref):
    nb = state_ref.shape[0]
    w_hist = w_ref[0:CONV_STATE, :]

    for b in range(nb):
        st = state_ref[b]
        conv_ref[b:b + 1, :] = jnp.sum(st * w_hist, axis=0, keepdims=True)
        ns_ref[b, 0:CONV_STATE - 1, :] = state_ref[b, 1:CONV_STATE, :]
        ns_ref[b, CONV_STATE - 1:CONV_STATE, :] = u_ref[b:b + 1, :]
    conv = conv_ref[...] + w_ref[CONV_STATE:CONV_WIDTH, :] * u_ref[...] + cb_ref[...]
    o_ref[...] = _ln_swish_gate(conv, g_ref, b_ref, gc_ref[...].astype(F32)).astype(o_ref.dtype)


def _conv_sample(state, u, gc, conv_w, conv_b, g_cn, b_cn):
    nb, _, dc = state.shape
    return pl.pallas_call(
        _conv_sample_kernel,
        out_shape=(jax.ShapeDtypeStruct((nb, dc), BF16),
                   jax.ShapeDtypeStruct((nb, CONV_STATE, dc), F32)),
        scratch_shapes=[pltpu.VMEM((nb, dc), F32)],
        compiler_params=pltpu.CompilerParams(vmem_limit_bytes=VMEM_LIMIT),
        name="conv_sample",
    )(state, u, gc, conv_w, conv_b, g_cn, b_cn)


def _lambda_full(lq1_ref, lk1_ref, lq2_ref, lk2_ref, layer_idx):
    s1 = jnp.sum(lq1_ref[...] * lk1_ref[...], axis=-1, keepdims=True)
    s2 = jnp.sum(lq2_ref[...] * lk2_ref[...], axis=-1, keepdims=True)
    return jnp.exp(s1) - jnp.exp(s2) + _lambda_init(layer_idx)


def _subln_gate(o, g_ref, gate, layer_idx):
    y = o * lax.rsqrt(jnp.mean(o * o, axis=-1, keepdims=True) + EPS) * g_ref[...]
    return y * (1.0 - _lambda_init(layer_idx)) * gate


def _trips_before(i):
    return (i * i) // 4


def _attn_kernel(pt_ref,
                 q_ref, k_ref, v_ref, ga_ref, g_ref, lq1_ref, lk1_ref, lq2_ref, lk2_ref,
                 qd_ref, knd_ref, vnd_ref, gad_ref, ck_hbm, cv_hbm,
                 o_ref, od_ref,
                 kb_ref, vt_ref, qs_ref, sa_ref, sb_ref, mxa_ref, mxb_ref, m_ref, acc_ref,
                 kpg_ref, vpg_ref, ksem, vsem, md_ref, ld_ref, accd_ref,
                 *, bq, layer_idx, pages, trips_per_seq, n_trips):
    h = pl.program_id(0)
    i = pl.program_id(1)
    n_chunks = kb_ref.shape[0]
    hd = q_ref.shape[1]
    half = hd // 2
    nh = qd_ref.shape[1]
    n_slots = kpg_ref.shape[0]
    trips_per_head = _trips_before(n_chunks)

    def page_copies(trip, slot, from_table):
        copies = []
        for j in range(pages):
            page = pt_ref[trip * pages + j] if from_table else 0
            copies.append(pltpu.make_async_copy(ck_hbm.at[page], kpg_ref.at[slot, j], ksem.at[slot]))
            copies.append(pltpu.make_async_copy(cv_hbm.at[page], vpg_ref.at[slot, j], vsem.at[slot]))
        return copies

    @pl.when(jnp.logical_and(h == 0, i == 0))
    def _():
        for trip in range(n_slots - 1):
            for cp in page_copies(trip, trip, True):
                cp.start()

    ones_rows = jnp.where(lax.broadcasted_iota(jnp.int32, (ONES_ROWS, bq), 0) == 0, 1.0, 0.0).astype(BF16)

    @pl.when(i == 0)
    def _():
        for c in range(n_chunks):
            kb_ref[c] = k_ref[c * bq:(c + 1) * bq, :].astype(BF16)
            vt_ref[c, 0:hd, :] = v_ref[c * bq:(c + 1) * bq, :].T.astype(BF16)
            vt_ref[c, hd:hd + ONES_ROWS, :] = ones_rows

    q_t = q_ref[...].astype(F32).T
    d_idx = lax.broadcasted_iota(jnp.int32, q_t.shape, 0)
    qs_ref[:, 0:bq] = jnp.where(d_idx < half, q_t, 0.0).astype(BF16)
    qs_ref[:, bq:2 * bq] = jnp.where(d_idx >= half, q_t, 0.0).astype(BF16)
    m_ref[...] = jnp.full(m_ref.shape, -jnp.inf, F32)
    acc_ref[...] = jnp.zeros(acc_ref.shape, F32)

    def scores(c, s_ref, mx_ref, diagonal=False):
        s = jnp.dot(kb_ref[c], qs_ref[...], preferred_element_type=F32)
        if diagonal:
            kidx = lax.broadcasted_iota(jnp.int32, s.shape, 0)
            qidx = lax.broadcasted_iota(jnp.int32, s.shape, 1) & (bq - 1)
            s = jnp.where(kidx <= qidx, s, -jnp.inf)
        s_ref[...] = s
        mx_ref[...] = jnp.max(s, axis=0, keepdims=True)

    def accumulate(c, s_ref, mx_ref):
        m_old = m_ref[...]
        m_new = jnp.maximum(m_old, mx_ref[...])
        p = jnp.exp2(s_ref[...] - m_new)
        alpha = jnp.exp2(m_old - m_new)
        pv = jnp.dot(vt_ref[c], p.astype(BF16), preferred_element_type=F32)
        acc_ref[...] = alpha * acc_ref[...] + pv
        m_ref[...] = m_new

    def decode_query(b):
        q = qd_ref[b]
        lane = lax.broadcasted_iota(jnp.int32, q.shape, 1)
        qx = jnp.concatenate([jnp.where(lane < half, q, 0.0), jnp.where(lane >= half, q, 0.0)], axis=0)
        return qx.astype(BF16)

    def decode_scores(b, slot):
        rows = pages * kpg_ref.shape[2]
        col_head = lax.broadcasted_iota(jnp.int32, (2 * nh, rows), 1) & (nh - 1)
        row_head = lax.broadcasted_iota(jnp.int32, (2 * nh, rows), 0) & (nh - 1)
        k_all = kpg_ref[slot].reshape(rows, hd).astype(BF16)
        s = lax.dot_general(decode_query(b), k_all, NT_DIMS, preferred_element_type=F32)
        return jnp.where(col_head == row_head, s, -jnp.inf)

    def decode_update(s, slot, first):
        m_old = jnp.where(first, -jnp.inf, md_ref[...])
        l_old = jnp.where(first, 0.0, ld_ref[...])
        acc_old = jnp.where(first, 0.0, accd_ref[...])
        m_new = jnp.maximum(m_old, jnp.max(s, axis=-1, keepdims=True))
        alpha = jnp.exp(m_old - m_new)
        p = jnp.exp(s - m_new)
        v_all = vpg_ref[slot].reshape(s.shape[1], hd).astype(BF16)
        md_ref[...] = m_new
        ld_ref[...] = alpha * l_old + jnp.sum(p, axis=-1, keepdims=True)
        accd_ref[...] = alpha * acc_old + jnp.dot(p.astype(BF16), v_all, preferred_element_type=F32)

    def decode_finish(b, lam):
        qx = decode_query(b).astype(F32)
        kn = knd_ref[b].astype(BF16).astype(F32)
        vn = vnd_ref[b].astype(BF16).astype(F32)
        kn2 = jnp.concatenate([kn, kn], axis=0)
        vn2 = jnp.concatenate([vn, vn], axis=0)
        s_new = jnp.sum(qx * kn2, axis=-1, keepdims=True)
        m_old = md_ref[...]
        m_fin = jnp.maximum(m_old, s_new)
        a_fin = jnp.exp(m_old - m_fin)
        p_new = jnp.exp(s_new - m_fin)
        l_fin = a_fin * ld_ref[...] + p_new
        on = (a_fin * accd_ref[...] + p_new * vn2) / l_fin
        o = on[0:nh] - lam * on[nh:2 * nh]
        od_ref[b] = _subln_gate(o, g_ref, gad_ref[b], layer_idx)

    scores(i, sa_ref, mxa_ref, diagonal=True)
    trip0 = h * trips_per_head + _trips_before(i)

    def body(p, carry):
        trip = trip0 + p
        slot = lax.rem(trip, n_slots)
        for cp in page_copies(trip, slot, False):
            cp.wait()

        @pl.when(trip + n_slots - 1 < n_trips)
        def _():
            nxt = trip + n_slots - 1
            for cp in page_copies(nxt, lax.rem(nxt, n_slots), True):
                cp.start()

        seq = trip // trips_per_seq
        part = lax.rem(trip, trips_per_seq)
        s_dec = decode_scores(seq, slot)
        scores(2 * p, sb_ref, mxb_ref)
        accumulate(jnp.where(p == 0, i, 2 * p - 1), sa_ref, mxa_ref)
        decode_update(s_dec, slot, part == 0)
        scores(jnp.minimum(2 * p + 1, i - 1), sa_ref, mxa_ref)
        accumulate(2 * p, sb_ref, mxb_ref)

        @pl.when(part == trips_per_seq - 1)
        def _():
            decode_finish(seq, _lambda_full(lq1_ref, lk1_ref, lq2_ref, lk2_ref, layer_idx))

        return carry

    lax.fori_loop(0, (i + 1) // 2, body, 0)

    @pl.when(i % 2 == 0)
    def _():
        accumulate(jnp.maximum(i - 1, 0), sa_ref, mxa_ref)

    lam = _lambda_full(lq1_ref, lk1_ref, lq2_ref, lk2_ref, layer_idx)
    acc = acc_ref[0:hd, :]
    inv_l = 1.0 / acc_ref[hd:hd + 1, :]
    o_t = acc[:, 0:bq] * inv_l[:, 0:bq] - lam * (acc[:, bq:2 * bq] * inv_l[:, bq:2 * bq])
    o = o_t.T
    o_ref[...] = _subln_gate(o, g_ref, ga_ref[...].astype(F32), layer_idx).astype(o_ref.dtype)


def _attention(q, k, v, ga, g_subln, lq1, lk1, lq2, lk2,
               page_table, q_d, k_new, v_new, ga_d, cache_k, cache_v, layer_idx, bq=512, n_slots=3):
    s, da = q.shape
    hd = da // N_HEADS
    nq = s // bq
    nb, n_pages = page_table.shape
    n_pool, page, nh, _ = cache_k.shape
    ck = cache_k.reshape(n_pool, page * nh, hd)
    cv = cache_v.reshape(n_pool, page * nh, hd)
    pt = page_table.reshape(nb * n_pages)

    n_trips = N_HEADS * _trips_before(nq)
    pages = (nb * n_pages) // n_trips
    assert pages * n_trips == nb * n_pages and n_pages % pages == 0, (n_trips, nb, n_pages)
    assert n_trips >= n_slots
    trips_per_seq = n_pages // pages

    kern = functools.partial(_attn_kernel, bq=bq, layer_idx=layer_idx, pages=pages,
                             trips_per_seq=trips_per_seq, n_trips=n_trips)
    tile = pl.BlockSpec((bq, hd), lambda h, i, pt: (i, h))
    full = pl.BlockSpec((s, hd), lambda h, i, pt: (0, h))
    small = lambda a: pl.BlockSpec(a.shape, lambda h, i, pt: (0, 0))
    whole = pl.BlockSpec((nb, nh, hd), lambda h, i, pt: (0, 0, 0))
    hbm = pl.BlockSpec(memory_space=pl.ANY)
    grid_spec = pltpu.PrefetchScalarGridSpec(
        num_scalar_prefetch=1,
        grid=(N_HEADS, nq),
        in_specs=[tile, full, full, tile, small(g_subln), small(lq1), small(lk1), small(lq2), small(lk2),
                  whole, whole, whole, whole, hbm, hbm],
        out_specs=(tile, whole),
        scratch_shapes=[pltpu.VMEM((nq, bq, hd), BF16),
                        pltpu.VMEM((nq, hd + ONES_ROWS, bq), BF16),
                        pltpu.VMEM((hd, 2 * bq), BF16),
                        pltpu.VMEM((bq, 2 * bq), F32), pltpu.VMEM((bq, 2 * bq), F32),
                        pltpu.VMEM((1, 2 * bq), F32), pltpu.VMEM((1, 2 * bq), F32),
                        pltpu.VMEM((1, 2 * bq), F32),
                        pltpu.VMEM((hd + ONES_ROWS, 2 * bq), F32),
                        pltpu.VMEM((n_slots, pages, page * nh, hd), F32),
                        pltpu.VMEM((n_slots, pages, page * nh, hd), F32),
                        pltpu.SemaphoreType.DMA((n_slots,)), pltpu.SemaphoreType.DMA((n_slots,)),
                        pltpu.VMEM((2 * nh, 1), F32), pltpu.VMEM((2 * nh, 1), F32),
                        pltpu.VMEM((2 * nh, hd), F32)],
    )
    return pl.pallas_call(
        kern,
        out_shape=(jax.ShapeDtypeStruct((s, da), BF16), jax.ShapeDtypeStruct((nb, nh, hd), F32)),
        grid_spec=grid_spec,
        compiler_params=_params("arbitrary", "arbitrary"),
        name="attention",
    )(pt, q, k, v, ga, g_subln, lq1, lk1, lq2, lk2,
      q_d.reshape(nb, nh, hd), k_new.reshape(nb, nh, hd), v_new.reshape(nb, nh, hd), ga_d.reshape(nb, nh, hd),
      ck, cv)


def _out_kernel(oc_ref, oa_ref, wc_ref, wa_ref, x_ref, gate_ref, g_ref, o_ref):
    y = jnp.dot(oc_ref[...].astype(BF16), wc_ref[...], preferred_element_type=F32)
    y = y + jnp.dot(oa_ref[...].astype(BF16), wa_ref[...], preferred_element_type=F32)
    yn = y * lax.rsqrt(jnp.mean(y * y, axis=-1, keepdims=True) + EPS) * g_ref[...]
    o_ref[...] = x_ref[...] + gate_ref[...] * yn


def _out_proj(oc, oa, w_out_bf16, x, gate, g_post, bm):
    m, d = x.shape
    dc = oc.shape[1]
    da = oa.shape[1]
    per_row = gate.shape[0] != 1
    assert dc % da == 0
    gate_spec = pl.BlockSpec((bm, d), lambda i: (i, 0)) if per_row else pl.BlockSpec((1, d), lambda i: (0, 0))
    return pl.pallas_call(
        _out_kernel,
        out_shape=jax.ShapeDtypeStruct((m, d), F32),
        grid=(m // bm,),
        in_specs=[pl.BlockSpec((bm, dc), lambda i: (i, 0)),
                  pl.BlockSpec((bm, da), lambda i: (i, 0)),
                  pl.BlockSpec((dc, d), lambda i: (0, 0)),
                  pl.BlockSpec((da, d), lambda i: (dc // da, 0)),
                  pl.BlockSpec((bm, d), lambda i: (i, 0)),
                  gate_spec,
                  pl.BlockSpec((1, d), lambda i: (0, 0))],
        out_specs=pl.BlockSpec((bm, d), lambda i: (i, 0)),
        compiler_params=_params("arbitrary"),
        name="out_proj",
    )(oc, oa, w_out_bf16, w_out_bf16, x, gate, g_post)


def kernel(x_prompt, x_sample, cache_k, cache_v, state_conv, page_table, c_prompt, c_sample,
           w_ada, b_ada, g_pre, g_post, w_in, conv_w, conv_b, g_cn, b_cn,
           lq1, lk1, lq2, lk2, g_subln, w_out):
    depth = w_ada.shape[0]
    bp, seq, d = x_prompt.shape
    nb = x_sample.shape[0]
    assert bp == 1 and x_sample.shape[1] == 1
    dc = conv_w.shape[2]
    da = w_out.shape[1] - dc
    hd = da // N_HEADS
    dk = hd // 2

    xp = x_prompt.reshape(seq, d)
    xs = x_sample.reshape(nb, d)
    pad = (-(bp + nb)) % 8
    c_all = jnp.concatenate([c_prompt, c_sample, jnp.zeros((pad, d), F32)], axis=0)
    row = lambda a: a.reshape(1, -1)

    kp_l, vp_l, cp_l, ks_l, vs_l, cs_l = [], [], [], [], [], []
    for l in range(depth):
        w_in_b = w_in[l].astype(BF16)
        w_out_b = w_out[l].astype(BF16)
        mod = _modulation(c_all, w_ada[l], row(b_ada[l]))
        shift, scale, gate = mod[:, 0:d], mod[:, d:2 * d], mod[:, 2 * d:3 * d]
        vecs = (row(conv_b[l]), row(g_cn[l]), row(b_cn[l]))
        lams = (row(lq1[l]), row(lk1[l]), row(lq2[l]), row(lk2[l]))
        g_sub = row(g_subln[l])
        conv_w8 = jnp.broadcast_to(conv_w[l][:, None, :], (CONV_WIDTH, SUBLANES, dc))

        u, gc, q, k, v, ga = _in_proj(xp, shift[0:1], scale[0:1], row(g_pre[l]), w_in_b, dc, da, bm=256,
                                      act_dtype=BF16, q_scale=dk ** -0.5 * math.log2(math.e))
        sl = slice(bp, bp + nb)
        us, gcs, qs, ksn, vsn, gas = _in_proj(xs, shift[sl], scale[sl], row(g_pre[l]), w_in_b, dc, da, bm=nb,
                                              act_dtype=F32, q_scale=dk ** -0.5)

        oc = _conv_prompt(u, gc, conv_w8, *vecs)
        ocs, new_state = _conv_sample(state_conv[l], us, gcs, conv_w[l], *vecs)

        oa, oas = _attention(q, k, v, ga, g_sub, *lams, page_table, qs, ksn, vsn, gas,
                             cache_k[l], cache_v[l], layer_idx=l)

        xp = _out_proj(oc, oa, w_out_b, xp, gate[0:1], row(g_post[l]), bm=256)
        xs = _out_proj(ocs, oas.reshape(nb, da), w_out_b, xs, gate[sl], row(g_post[l]), bm=nb)
        kp_l.append(k.reshape(bp, seq, N_HEADS, hd))
        vp_l.append(v.reshape(bp, seq, N_HEADS, hd))
        cp_l.append(u[seq - CONV_STATE:].reshape(bp, CONV_STATE, dc))
        ks_l.append(ksn.reshape(nb, 1, N_HEADS, hd))
        vs_l.append(vsn.reshape(nb, 1, N_HEADS, hd))
        cs_l.append(new_state)

    return (xp.reshape(bp, seq, d), xs.reshape(nb, 1, d),
            jnp.stack(kp_l), jnp.stack(vp_l), jnp.stack(cp_l),
            jnp.stack(ks_l), jnp.stack(vs_l), jnp.stack(cs_l))
```

```python
import functools
import math

import jax
import jax.numpy as jnp
from jax import lax
from jax.experimental import pallas as pl
from jax.experimental.pallas import tpu as pltpu

F32 = jnp.float32
BF16 = jnp.bfloat16

N_HEADS = 8
CONV_WIDTH = 31
CONV_STATE = CONV_WIDTH - 1
CONV_HALO = 32
EPS = 1e-6

LANES = 128
SUBLANES = 8
ONES_ROWS = 16
VMEM_LIMIT = 56 * 1024 * 1024

NT_DIMS = (((1,), (1,)), ((), ()))


def _lambda_init(layer_idx):
    return 0.8 - 0.6 * math.exp(-0.3 * layer_idx)


def _params(*sem):
    return pltpu.CompilerParams(dimension_semantics=sem, vmem_limit_bytes=VMEM_LIMIT)


def _sigmoid(x):
    return 1.0 / (1.0 + jnp.exp(-x))


def _silu(x):
    return x * _sigmoid(x)


def _mod_kernel(c_ref, w_ref, b_ref, o_ref):
    a = _silu(c_ref[...]).astype(BF16)
    o_ref[...] = jnp.dot(a, w_ref[...].astype(BF16), preferred_element_type=F32) + b_ref[...]


def _modulation(c_all, w_ada, b_ada, bn=512):
    rows, d = c_all.shape
    n = w_ada.shape[1]
    return pl.pallas_call(
        _mod_kernel,
        out_shape=jax.ShapeDtypeStruct((rows, n), F32),
        grid=(n // bn,),
        in_specs=[pl.BlockSpec((rows, d), lambda j: (0, 0)),
                  pl.BlockSpec((d, bn), lambda j: (0, j)),
                  pl.BlockSpec((1, bn), lambda j: (0, j))],
        out_specs=pl.BlockSpec((rows, bn), lambda j: (0, j)),
        compiler_params=_params("arbitrary"),
        name="adaln_mod",
    )(c_all, w_ada, b_ada)


def _ln_swish_gate(conv, g_ref, b_ref, gate):
    mu = jnp.mean(conv, axis=-1, keepdims=True)
    d = conv - mu
    var = jnp.mean(d * d, axis=-1, keepdims=True)
    y = d * lax.rsqrt(var + EPS) * g_ref[...] + b_ref[...]
    return _silu(y) * gate


def _causal_conv_rows(ext_ref, sh_ref, conv_ref, w_ref, cb_ref, row0, bt, rows=32):
    span = sh_ref.shape[1]
    for b in range(1, SUBLANES):
        sh_ref[b - 1] = ext_ref[row0 + b:row0 + b + span, :]
    off = CONV_HALO - CONV_STATE
    reps = rows // SUBLANES
    for c in range(ext_ref.shape[1] // LANES):
        cs = slice(c * LANES, (c + 1) * LANES)
        taps = [jnp.concatenate([w_ref[j, :, cs]] * reps, axis=0) for j in range(CONV_WIDTH)]
        bias = jnp.broadcast_to(cb_ref[:, cs], (rows, LANES))
        for r in range(bt // rows):
            acc = bias
            for j in range(CONV_WIDTH):
                a, b = divmod(off + j, SUBLANES)
                base = r * rows + SUBLANES * a
                if b == 0:
                    src = ext_ref[row0 + base:row0 + base + rows, cs]
                else:
                    src = sh_ref[b - 1, base:base + rows, cs]
                acc = acc + taps[j] * src
            conv_ref[row0 + r * rows:row0 + (r + 1) * rows, cs] = acc


def _modulated_norm(x_ref, shift_ref, scale_ref, g_ref):
    x = x_ref[...]
    y = x * lax.rsqrt(jnp.mean(x * x, axis=-1, keepdims=True) + EPS) * g_ref[...]
    return (y * (1.0 + scale_ref[...]) + shift_ref[...]).astype(BF16)


def _proj_seq_kernel(x_ref, shift_ref, scale_ref, g_ref, w_ref, cw_ref, cb_ref, gcn_ref, bcn_ref,
                     oc_ref, utail_ref, q_ref, k_ref, v_ref, ga_ref, ext_ref, sh_ref, conv_ref,
                     *, dc, da, q_scale):
    h = _modulated_norm(x_ref, shift_ref, scale_ref, g_ref)
    bm = h.shape[0]

    def seg(start, width):
        return jnp.dot(h, w_ref[:, start:start + width], preferred_element_type=F32)

    u = seg(0, dc) * _sigmoid(seg(dc, dc))
    gate_c = _silu(seg(2 * dc, dc))

    @pl.when(pl.program_id(0) == 0)
    def _():
        ext_ref[0:CONV_HALO, :] = jnp.zeros((CONV_HALO, dc), F32)

    ext_ref[CONV_HALO:CONV_HALO + bm, :] = u
    utail_ref[...] = u[bm - CONV_HALO:bm, :]
    bt = sh_ref.shape[1] + SUBLANES - CONV_HALO
    for row0 in range(0, bm, bt):
        _causal_conv_rows(ext_ref, sh_ref, conv_ref, cw_ref, cb_ref, row0, bt)
    oc_ref[...] = _ln_swish_gate(conv_ref[...], gcn_ref, bcn_ref, gate_c).astype(oc_ref.dtype)
    ext_ref[0:CONV_HALO, :] = ext_ref[bm:bm + CONV_HALO, :]

    q_ref[...] = (seg(3 * dc, da) * q_scale).astype(q_ref.dtype)
    k_ref[...] = seg(3 * dc + da, da)
    v_ref[...] = seg(3 * dc + 2 * da, da)
    ga_ref[...] = _silu(seg(3 * dc + 3 * da, da)).astype(ga_ref.dtype)


def _proj_rows_kernel(x_ref, shift_ref, scale_ref, g_ref, w_ref, o_ref):
    h = _modulated_norm(x_ref, shift_ref, scale_ref, g_ref)
    o_ref[...] = jnp.dot(h, w_ref[...], preferred_element_type=F32)


def _in_proj_seq(x, shift, scale, g_pre, w_in_bf16, conv_w8, conv_b, g_cn, b_cn, dc, da, q_scale,
                 bm=256, conv_bt=128):
    m, d = x.shape
    n = w_in_bf16.shape[1]
    assert bm % conv_bt == 0 and shift.shape[0] == 1
    kern = functools.partial(_proj_seq_kernel, dc=dc, da=da, q_scale=q_scale)
    row = lambda w: pl.BlockSpec((bm, w), lambda i: (i, 0))
    const = lambda a: pl.BlockSpec(a.shape, lambda i: (0,) * a.ndim)
    return pl.pallas_call(
        kern,
        out_shape=(jax.ShapeDtypeStruct((m, dc), BF16), jax.ShapeDtypeStruct((CONV_HALO, dc), F32),
                   jax.ShapeDtypeStruct((m, da), BF16), jax.ShapeDtypeStruct((m, da), F32),
                   jax.ShapeDtypeStruct((m, da), F32), jax.ShapeDtypeStruct((m, da), BF16)),
        grid=(m // bm,),
        in_specs=[row(d), const(shift), const(scale), const(g_pre),
                  pl.BlockSpec((d, n), lambda i: (0, 0), pipeline_mode=pl.Buffered(1)),
                  const(conv_w8), const(conv_b), const(g_cn), const(b_cn)],
        out_specs=(row(dc), pl.BlockSpec((CONV_HALO, dc), lambda i: (0, 0)),
                   row(da), row(da), row(da), row(da)),
        scratch_shapes=[pltpu.VMEM((bm + CONV_HALO, dc), F32),
                        pltpu.VMEM((SUBLANES - 1, conv_bt + CONV_HALO - SUBLANES, dc), F32),
                        pltpu.VMEM((bm, dc), F32)],
        compiler_params=_params("arbitrary"),
        name="in_proj_seq",
    )(x, shift, scale, g_pre, w_in_bf16, conv_w8, conv_b, g_cn, b_cn)


def _in_proj_rows(x, shift, scale, g_pre, w_in_bf16, bn):
    m, d = x.shape
    n = w_in_bf16.shape[1]
    const = lambda a: pl.BlockSpec(a.shape, lambda j: (0,) * a.ndim)
    return pl.pallas_call(
        _proj_rows_kernel,
        out_shape=jax.ShapeDtypeStruct((m, n), F32),
        grid=(n // bn,),
        in_specs=[const(x), const(shift), const(scale), const(g_pre), pl.BlockSpec((d, bn), lambda j: (0, j))],
        out_specs=pl.BlockSpec((m, bn), lambda j: (0, j)),
        compiler_params=_params("arbitrary"),
        name="in_proj_rows",
    )(x, shift, scale, g_pre, w_in_bf16)


def _conv_sample_kernel(state_ref, a_ref, b_ref, z_ref, w_ref, cb_ref, gcn_ref, bcn_ref,
                        o_ref, ns_ref, conv_ref):
    nb = state_ref.shape[0]
    w_hist = w_ref[0:CONV_STATE, :]
    u = a_ref[...] * _sigmoid(b_ref[...])
    for b in range(nb):
        st = state_ref[b]
        conv_ref[b:b + 1, :] = jnp.sum(st * w_hist, axis=0, keepdims=True)
        ns_ref[b, 0:CONV_STATE - 1, :] = state_ref[b, 1:CONV_STATE, :]
        ns_ref[b, CONV_STATE - 1:CONV_STATE, :] = u[b:b + 1, :]
    conv = conv_ref[...] + w_ref[CONV_STATE:CONV_WIDTH, :] * u + cb_ref[...]
    o_ref[...] = _ln_swish_gate(conv, gcn_ref, bcn_ref, _silu(z_ref[...])).astype(o_ref.dtype)


def _conv_sample(state, proj, conv_w, conv_b, g_cn, b_cn):
    nb, _, dc = state.shape
    const = lambda a: pl.BlockSpec(a.shape, lambda i: (0,) * a.ndim)
    col = lambda c: pl.BlockSpec((nb, dc), lambda i: (0, c))
    return pl.pallas_call(
        _conv_sample_kernel,
        out_shape=(jax.ShapeDtypeStruct((nb, dc), BF16),
                   jax.ShapeDtypeStruct((nb, CONV_STATE, dc), F32)),
        grid=(1,),
        in_specs=[const(state), col(0), col(1), col(2), const(conv_w), const(conv_b), const(g_cn), const(b_cn)],
        out_specs=(pl.BlockSpec((nb, dc), lambda i: (0, 0)),
                   pl.BlockSpec((nb, CONV_STATE, dc), lambda i: (0, 0, 0))),
        scratch_shapes=[pltpu.VMEM((nb, dc), F32)],
        compiler_params=_params("arbitrary"),
        name="conv_sample",
    )(state, proj, proj, proj, conv_w, conv_b, g_cn, b_cn)


def _lambda_full(lq1_ref, lk1_ref, lq2_ref, lk2_ref, layer_idx):
    s1 = jnp.sum(lq1_ref[...] * lk1_ref[...], axis=-1, keepdims=True)
    s2 = jnp.sum(lq2_ref[...] * lk2_ref[...], axis=-1, keepdims=True)
    return jnp.exp(s1) - jnp.exp(s2) + _lambda_init(layer_idx)


def _subln_gate(o, g_ref, gate, layer_idx):
    y = o * lax.rsqrt(jnp.mean(o * o, axis=-1, keepdims=True) + EPS) * g_ref[...]
    return y * (1.0 - _lambda_init(layer_idx)) * gate


def _trips_before(i):
    return (i * i) // 4


def _attn_kernel(pt_ref,
                 q_ref, k_ref, v_ref, ga_ref, g_ref, lq1_ref, lk1_ref, lq2_ref, lk2_ref,
                 qd_ref, knd_ref, vnd_ref, zd_ref, ck_hbm, cv_hbm,
                 o_ref, od_ref,
                 kb_ref, vt_ref, qs_ref, sa_ref, sb_ref, mxa_ref, mxb_ref, m_ref, acc_ref,
                 kpg_ref, vpg_ref, ksem, vsem, md_ref, ld_ref, accd_ref,
                 *, bq, layer_idx, pages, trips_per_seq, n_trips):
    h = pl.program_id(0)
    i = pl.program_id(1)
    n_chunks = kb_ref.shape[0]
    hd = q_ref.shape[1]
    half = hd // 2
    nh = qd_ref.shape[1]
    n_slots = kpg_ref.shape[0]
    trips_per_head = _trips_before(n_chunks)

    def page_copies(trip, slot, from_table):
        copies = []
        for j in range(pages):
            page = pt_ref[trip * pages + j] if from_table else 0
            copies.append(pltpu.make_async_copy(ck_hbm.at[page], kpg_ref.at[slot, j], ksem.at[slot]))
            copies.append(pltpu.make_async_copy(cv_hbm.at[page], vpg_ref.at[slot, j], vsem.at[slot]))
        return copies

    @pl.when(jnp.logical_and(h == 0, i == 0))
    def _():
        for trip in range(n_slots - 1):
            for cp in page_copies(trip, trip, True):
                cp.start()

    ones_rows = jnp.where(lax.broadcasted_iota(jnp.int32, (ONES_ROWS, bq), 0) == 0, 1.0, 0.0).astype(BF16)

    @pl.when(i == 0)
    def _():
        for c in range(n_chunks):
            kb_ref[c] = k_ref[c * bq:(c + 1) * bq, :].astype(BF16)
            vt_ref[c, 0:hd, :] = v_ref[c * bq:(c + 1) * bq, :].T.astype(BF16)
            vt_ref[c, hd:hd + ONES_ROWS, :] = ones_rows

    q_t = q_ref[...].astype(F32).T
    d_idx = lax.broadcasted_iota(jnp.int32, q_t.shape, 0)
    qs_ref[:, 0:bq] = jnp.where(d_idx < half, q_t, 0.0).astype(BF16)
    qs_ref[:, bq:2 * bq] = jnp.where(d_idx >= half, q_t, 0.0).astype(BF16)
    m_ref[...] = jnp.full(m_ref.shape, -jnp.inf, F32)
    acc_ref[...] = jnp.zeros(acc_ref.shape, F32)

    def scores(c, s_ref, mx_ref, diagonal=False):
        s = jnp.dot(kb_ref[c], qs_ref[...], preferred_element_type=F32)
        if diagonal:
            kidx = lax.broadcasted_iota(jnp.int32, s.shape, 0)
            qidx = lax.broadcasted_iota(jnp.int32, s.shape, 1) & (bq - 1)
            s = jnp.where(kidx <= qidx, s, -jnp.inf)
        s_ref[...] = s
        mx_ref[...] = jnp.max(s, axis=0, keepdims=True)

    def accumulate(c, s_ref, mx_ref):
        m_old = m_ref[...]
        m_new = jnp.maximum(m_old, mx_ref[...])
        p = jnp.exp2(s_ref[...] - m_new)
        alpha = jnp.exp2(m_old - m_new)
        pv = jnp.dot(vt_ref[c], p.astype(BF16), preferred_element_type=F32)
        acc_ref[...] = alpha * acc_ref[...] + pv
        m_ref[...] = m_new

    def decode_query(b):
        q = qd_ref[b] * half ** -0.5
        lane = lax.broadcasted_iota(jnp.int32, q.shape, 1)
        qx = jnp.concatenate([jnp.where(lane < half, q, 0.0), jnp.where(lane >= half, q, 0.0)], axis=0)
        return qx.astype(BF16)

    def decode_scores(b, slot):
        rows = pages * kpg_ref.shape[2]
        col_head = lax.broadcasted_iota(jnp.int32, (2 * nh, rows), 1) & (nh - 1)
        row_head = lax.broadcasted_iota(jnp.int32, (2 * nh, rows), 0) & (nh - 1)
        k_all = kpg_ref[slot].reshape(rows, hd).astype(BF16)
        s = lax.dot_general(decode_query(b), k_all, NT_DIMS, preferred_element_type=F32)
        return jnp.where(col_head == row_head, s, -jnp.inf)

    def decode_update(s, slot, first):
        m_old = jnp.where(first, -jnp.inf, md_ref[...])
        l_old = jnp.where(first, 0.0, ld_ref[...])
        acc_old = jnp.where(first, 0.0, accd_ref[...])
        m_new = jnp.maximum(m_old, jnp.max(s, axis=-1, keepdims=True))
        alpha = jnp.exp(m_old - m_new)
        p = jnp.exp(s - m_new)
        v_all = vpg_ref[slot].reshape(s.shape[1], hd).astype(BF16)
        md_ref[...] = m_new
        ld_ref[...] = alpha * l_old + jnp.sum(p, axis=-1, keepdims=True)
        accd_ref[...] = alpha * acc_old + jnp.dot(p.astype(BF16), v_all, preferred_element_type=F32)

    def decode_finish(b, lam):
        qx = decode_query(b).astype(F32)
        kn = knd_ref[b].astype(BF16).astype(F32)
        vn = vnd_ref[b].astype(BF16).astype(F32)
        kn2 = jnp.concatenate([kn, kn], axis=0)
        vn2 = jnp.concatenate([vn, vn], axis=0)
        s_new = jnp.sum(qx * kn2, axis=-1, keepdims=True)
        m_old = md_ref[...]
        m_fin = jnp.maximum(m_old, s_new)
        a_fin = jnp.exp(m_old - m_fin)
        p_new = jnp.exp(s_new - m_fin)
        l_fin = a_fin * ld_ref[...] + p_new
        on = (a_fin * accd_ref[...] + p_new * vn2) / l_fin
        o = on[0:nh] - lam * on[nh:2 * nh]
        od_ref[b] = _subln_gate(o, g_ref, _silu(zd_ref[b]), layer_idx)

    scores(i, sa_ref, mxa_ref, diagonal=True)
    trip0 = h * trips_per_head + _trips_before(i)

    def body(p, carry):
        trip = trip0 + p
        slot = lax.rem(trip, n_slots)
        for cp in page_copies(trip, slot, False):
            cp.wait()

        @pl.when(trip + n_slots - 1 < n_trips)
        def _():
            nxt = trip + n_slots - 1
            for cp in page_copies(nxt, lax.rem(nxt, n_slots), True):
                cp.start()

        seq = trip // trips_per_seq
        part = lax.rem(trip, trips_per_seq)
        s_dec = decode_scores(seq, slot)
        scores(2 * p, sb_ref, mxb_ref)
        accumulate(jnp.where(p == 0, i, 2 * p - 1), sa_ref, mxa_ref)
        decode_update(s_dec, slot, part == 0)
        scores(jnp.minimum(2 * p + 1, i - 1), sa_ref, mxa_ref)
        accumulate(2 * p, sb_ref, mxb_ref)

        @pl.when(part == trips_per_seq - 1)
        def _():
            decode_finish(seq, _lambda_full(lq1_ref, lk1_ref, lq2_ref, lk2_ref, layer_idx))

        return carry

    lax.fori_loop(0, (i + 1) // 2, body, 0)

    @pl.when(i % 2 == 0)
    def _():
        accumulate(jnp.maximum(i - 1, 0), sa_ref, mxa_ref)

    lam = _lambda_full(lq1_ref, lk1_ref, lq2_ref, lk2_ref, layer_idx)
    acc = acc_ref[0:hd, :]
    inv_l = 1.0 / acc_ref[hd:hd + 1, :]
    o_t = acc[:, 0:bq] * inv_l[:, 0:bq] - lam * (acc[:, bq:2 * bq] * inv_l[:, bq:2 * bq])
    o = o_t.T
    o_ref[...] = _subln_gate(o, g_ref, ga_ref[...].astype(F32), layer_idx).astype(o_ref.dtype)


def _attention(q, k, v, ga, g_subln, lq1, lk1, lq2, lk2,
               page_table, proj_d, att_col, cache_k, cache_v, layer_idx, bq=512, n_slots=3):
    s, da = q.shape
    hd = da // N_HEADS
    nq = s // bq
    nb, n_pages = page_table.shape
    n_pool, page, nh, _ = cache_k.shape
    ck = cache_k.reshape(n_pool, page * nh, hd)
    cv = cache_v.reshape(n_pool, page * nh, hd)
    pt = page_table.reshape(nb * n_pages)
    assert att_col % da == 0
    proj_heads = proj_d.reshape(nb, proj_d.shape[1] // hd, hd)
    first_blk = att_col // da

    n_trips = N_HEADS * _trips_before(nq)
    pages = (nb * n_pages) // n_trips
    assert pages * n_trips == nb * n_pages and n_pages % pages == 0, (n_trips, nb, n_pages)
    assert n_trips >= n_slots
    trips_per_seq = n_pages // pages

    kern = functools.partial(_attn_kernel, bq=bq, layer_idx=layer_idx, pages=pages,
                             trips_per_seq=trips_per_seq, n_trips=n_trips)
    tile = pl.BlockSpec((bq, hd), lambda h, i, pt: (i, h))
    full = pl.BlockSpec((s, hd), lambda h, i, pt: (0, h))
    small = lambda a: pl.BlockSpec(a.shape, lambda h, i, pt: (0, 0))
    whole = pl.BlockSpec((nb, nh, hd), lambda h, i, pt: (0, 0, 0))
    dec = lambda blk: pl.BlockSpec((nb, nh, hd), lambda h, i, pt: (0, first_blk + blk, 0))
    hbm = pl.BlockSpec(memory_space=pl.ANY)
    grid_spec = pltpu.PrefetchScalarGridSpec(
        num_scalar_prefetch=1,
        grid=(N_HEADS, nq),
        in_specs=[tile, full, full, tile, small(g_subln), small(lq1), small(lk1), small(lq2), small(lk2),
                  dec(0), dec(1), dec(2), dec(3), hbm, hbm],
        out_specs=(tile, whole),
        scratch_shapes=[pltpu.VMEM((nq, bq, hd), BF16),
                        pltpu.VMEM((nq, hd + ONES_ROWS, bq), BF16),
                        pltpu.VMEM((hd, 2 * bq), BF16),
                        pltpu.VMEM((bq, 2 * bq), F32), pltpu.VMEM((bq, 2 * bq), F32),
                        pltpu.VMEM((1, 2 * bq), F32), pltpu.VMEM((1, 2 * bq), F32),
                        pltpu.VMEM((1, 2 * bq), F32),
                        pltpu.VMEM((hd + ONES_ROWS, 2 * bq), F32),
                        pltpu.VMEM((n_slots, pages, page * nh, hd), F32),
                        pltpu.VMEM((n_slots, pages, page * nh, hd), F32),
                        pltpu.SemaphoreType.DMA((n_slots,)), pltpu.SemaphoreType.DMA((n_slots,)),
                        pltpu.VMEM((2 * nh, 1), F32), pltpu.VMEM((2 * nh, 1), F32),
                        pltpu.VMEM((2 * nh, hd), F32)],
    )
    return pl.pallas_call(
        kern,
        out_shape=(jax.ShapeDtypeStruct((s, da), BF16), jax.ShapeDtypeStruct((nb, nh, hd), F32)),
        grid_spec=grid_spec,
        compiler_params=_params("arbitrary", "arbitrary"),
        name="attention",
    )(pt, q, k, v, ga, g_subln, lq1, lk1, lq2, lk2,
      proj_heads, proj_heads, proj_heads, proj_heads, ck, cv)


def _out_kernel(oc_ref, oa_ref, wc_ref, wa_ref, x_ref, gate_ref, g_ref, o_ref):
    y = jnp.dot(oc_ref[...].astype(BF16), wc_ref[...], preferred_element_type=F32)
    y = y + jnp.dot(oa_ref[...].astype(BF16), wa_ref[...], preferred_element_type=F32)
    yn = y * lax.rsqrt(jnp.mean(y * y, axis=-1, keepdims=True) + EPS) * g_ref[...]
    o_ref[...] = x_ref[...] + gate_ref[...] * yn


def _out_proj(oc, oa, w_out_bf16, x, gate, g_post, bm):
    m, d = x.shape
    dc = oc.shape[1]
    da = oa.shape[1]
    per_row = gate.shape[0] != 1
    assert dc % da == 0
    gate_spec = pl.BlockSpec((bm, d), lambda i: (i, 0)) if per_row else pl.BlockSpec((1, d), lambda i: (0, 0))
    return pl.pallas_call(
        _out_kernel,
        out_shape=jax.ShapeDtypeStruct((m, d), F32),
        grid=(m // bm,),
        in_specs=[pl.BlockSpec((bm, dc), lambda i: (i, 0)),
                  pl.BlockSpec((bm, da), lambda i: (i, 0)),
                  pl.BlockSpec((dc, d), lambda i: (0, 0)),
                  pl.BlockSpec((da, d), lambda i: (dc // da, 0)),
                  pl.BlockSpec((bm, d), lambda i: (i, 0)),
                  gate_spec,
                  pl.BlockSpec((1, d), lambda i: (0, 0))],
        out_specs=pl.BlockSpec((bm, d), lambda i: (i, 0)),
        compiler_params=_params("arbitrary"),
        name="out_proj",
    )(oc, oa, w_out_bf16, w_out_bf16, x, gate, g_post)


def kernel(x_prompt, x_sample, cache_k, cache_v, state_conv, page_table, c_prompt, c_sample,
           w_ada, b_ada, g_pre, g_post, w_in, conv_w, conv_b, g_cn, b_cn,
           lq1, lk1, lq2, lk2, g_subln, w_out):
    depth = w_ada.shape[0]
    bp, seq, d = x_prompt.shape
    nb = x_sample.shape[0]
    assert bp == 1 and x_sample.shape[1] == 1
    dc = conv_w.shape[2]
    da = w_out.shape[1] - dc
    hd = da // N_HEADS
    dk = hd // 2

    xp = x_prompt.reshape(seq, d)
    xs = x_sample.reshape(nb, d)
    pad = (-(bp + nb)) % 8
    c_all = jnp.concatenate([c_prompt, c_sample, jnp.zeros((pad, d), F32)], axis=0)
    row = lambda a: a.reshape(1, -1)

    kp_l, vp_l, cp_l, ks_l, vs_l, cs_l = [], [], [], [], [], []
    for l in range(depth):
        w_in_b = w_in[l].astype(BF16)
        w_out_b = w_out[l].astype(BF16)
        mod = _modulation(c_all, w_ada[l], row(b_ada[l]))
        shift, scale, gate = mod[:, 0:d], mod[:, d:2 * d], mod[:, 2 * d:3 * d]
        vecs = (row(conv_b[l]), row(g_cn[l]), row(b_cn[l]))
        lams = (row(lq1[l]), row(lk1[l]), row(lq2[l]), row(lk2[l]))
        g_sub = row(g_subln[l])
        conv_w8 = jnp.broadcast_to(conv_w[l][:, None, :], (CONV_WIDTH, SUBLANES, dc))

        oc, u_tail, q, k, v, ga = _in_proj_seq(xp, shift[0:1], scale[0:1], row(g_pre[l]), w_in_b, conv_w8, *vecs,
                                               dc=dc, da=da, q_scale=dk ** -0.5 * math.log2(math.e))
        sl = slice(bp, bp + nb)
        proj_s = _in_proj_rows(xs, shift[sl], scale[sl], row(g_pre[l]), w_in_b, bn=dc)
        att_col = 3 * dc
        ksn = proj_s[:, att_col + da:att_col + 2 * da]
        vsn = proj_s[:, att_col + 2 * da:att_col + 3 * da]

        ocs, new_state = _conv_sample(state_conv[l], proj_s, conv_w[l], *vecs)

        oa, oas = _attention(q, k, v, ga, g_sub, *lams, page_table, proj_s, att_col,
                             cache_k[l], cache_v[l], layer_idx=l)

        xp = _out_proj(oc, oa, w_out_b, xp, gate[0:1], row(g_post[l]), bm=512)
        xs = _out_proj(ocs, oas.reshape(nb, da), w_out_b, xs, gate[sl], row(g_post[l]), bm=nb)
        kp_l.append(k.reshape(bp, seq, N_HEADS, hd))
        vp_l.append(v.reshape(bp, seq, N_HEADS, hd))
        cp_l.append(u_tail[CONV_HALO - CONV_STATE:].reshape(bp, CONV_STATE, dc))
        ks_l.append(ksn.reshape(nb, 1, N_HEADS, hd))
        vs_l.append(vsn.reshape(nb, 1, N_HEADS, hd))
        cs_l.append(new_state)

    return (xp.reshape(bp, seq, d), xs.reshape(nb, 1, d),
            jnp.stack(kp_l), jnp.stack(vp_l), jnp.stack(cp_l),
            jnp.stack(ks_l), jnp.stack(vs_l), jnp.stack(cs_l))
```

```python
import functools
import math

import jax
import jax.numpy as jnp
from jax import lax
from jax.experimental import pallas as pl
from jax.experimental.pallas import tpu as pltpu

F32 = jnp.float32
BF16 = jnp.bfloat16

N_HEADS = 8
CONV_WIDTH = 31
CONV_STATE = CONV_WIDTH - 1
CONV_HALO = 32
EPS = 1e-6

LANES = 128
SUBLANES = 8
MXU_TILE = 256
ONES_ROWS = 16
VMEM_LIMIT = 56 * 1024 * 1024

NT_DIMS = (((1,), (1,)), ((), ()))


def _lambda_init(layer_idx):
    return 0.8 - 0.6 * math.exp(-0.3 * layer_idx)


def _params(*sem):
    return pltpu.CompilerParams(dimension_semantics=sem, vmem_limit_bytes=VMEM_LIMIT)


def _sigmoid(x):
    return 1.0 / (1.0 + jnp.exp(-x))


def _silu(x):
    return x * _sigmoid(x)


def _mod_kernel(c_ref, w_ref, b_ref, o_ref):
    a = _silu(c_ref[...]).astype(BF16)
    o_ref[...] = jnp.dot(a, w_ref[...].astype(BF16), preferred_element_type=F32) + b_ref[...]


def _modulation(c_all, w_ada, b_ada, bn=512):
    rows, d = c_all.shape
    n = w_ada.shape[1]
    return pl.pallas_call(
        _mod_kernel,
        out_shape=jax.ShapeDtypeStruct((rows, n), F32),
        grid=(n // bn,),
        in_specs=[pl.BlockSpec((rows, d), lambda j: (0, 0)),
                  pl.BlockSpec((d, bn), lambda j: (0, j)),
                  pl.BlockSpec((1, bn), lambda j: (0, j))],
        out_specs=pl.BlockSpec((rows, bn), lambda j: (0, j)),
        compiler_params=_params("arbitrary"),
        name="adaln_mod",
    )(c_all, w_ada, b_ada)


def _ln_swish_gate(conv, g_ref, b_ref, gate):
    mu = jnp.mean(conv, axis=-1, keepdims=True)
    d = conv - mu
    var = jnp.mean(d * d, axis=-1, keepdims=True)
    y = d * lax.rsqrt(var + EPS) * g_ref[...] + b_ref[...]
    return _silu(y) * gate


def _causal_conv_rows(ext_ref, sh_ref, conv_ref, w_ref, cb_ref, row0, bt, rows=32):
    span = sh_ref.shape[1]
    for b in range(1, SUBLANES):
        sh_ref[b - 1] = ext_ref[row0 + b:row0 + b + span, :]
    off = CONV_HALO - CONV_STATE
    reps = rows // SUBLANES
    for c in range(ext_ref.shape[1] // LANES):
        cs = slice(c * LANES, (c + 1) * LANES)
        taps = [jnp.concatenate([w_ref[j, :, cs]] * reps, axis=0) for j in range(CONV_WIDTH)]
        bias = jnp.broadcast_to(cb_ref[:, cs], (rows, LANES))
        for r in range(bt // rows):
            acc = bias
            for j in range(CONV_WIDTH):
                a, b = divmod(off + j, SUBLANES)
                base = r * rows + SUBLANES * a
                if b == 0:
                    src = ext_ref[row0 + base:row0 + base + rows, cs]
                else:
                    src = sh_ref[b - 1, base:base + rows, cs]
                acc = acc + taps[j] * src
            conv_ref[row0 + r * rows:row0 + (r + 1) * rows, cs] = acc


def _modulated_norm(x_ref, shift_ref, scale_ref, g_ref):
    x = x_ref[...]
    y = x * lax.rsqrt(jnp.mean(x * x, axis=-1, keepdims=True) + EPS) * g_ref[...]
    return (y * (1.0 + scale_ref[...]) + shift_ref[...]).astype(BF16)


def _proj_seq_kernel(x_ref, shift_ref, scale_ref, g_ref, w_ref, cw_ref, cb_ref, gcn_ref, bcn_ref,
                     oc_ref, utail_ref, q_ref, k_ref, v_ref, ga_ref, ext_ref, sh_ref, conv_ref,
                     *, dc, da, q_scale):
    h = _modulated_norm(x_ref, shift_ref, scale_ref, g_ref)
    bm = h.shape[0]

    def seg(start, width):
        return jnp.dot(h, w_ref[:, start:start + width], preferred_element_type=F32)

    u = seg(0, dc) * _sigmoid(seg(dc, dc))
    gate_c = _silu(seg(2 * dc, dc))

    @pl.when(pl.program_id(0) == 0)
    def _():
        ext_ref[0:CONV_HALO, :] = jnp.zeros((CONV_HALO, dc), F32)

    ext_ref[CONV_HALO:CONV_HALO + bm, :] = u
    utail_ref[...] = u[bm - CONV_HALO:bm, :]
    bt = sh_ref.shape[1] + SUBLANES - CONV_HALO
    for row0 in range(0, bm, bt):
        _causal_conv_rows(ext_ref, sh_ref, conv_ref, cw_ref, cb_ref, row0, bt)
    oc_ref[...] = _ln_swish_gate(conv_ref[...], gcn_ref, bcn_ref, gate_c).astype(oc_ref.dtype)
    ext_ref[0:CONV_HALO, :] = ext_ref[bm:bm + CONV_HALO, :]

    q_ref[...] = (seg(3 * dc, da) * q_scale).astype(q_ref.dtype)
    k_ref[...] = seg(3 * dc + da, da)
    v_ref[...] = seg(3 * dc + 2 * da, da)
    ga_ref[...] = _silu(seg(3 * dc + 3 * da, da)).astype(ga_ref.dtype)


def _proj_rows_kernel(x_ref, shift_ref, scale_ref, g_ref, w_ref, o_ref, h_ref):
    kt = pl.program_id(0)
    n_kt, _, bk = h_ref.shape

    @pl.when(kt == 0)
    def _():
        h = _modulated_norm(x_ref, shift_ref, scale_ref, g_ref)
        for c in range(n_kt):
            h_ref[c] = h[:, c * bk:(c + 1) * bk]
        o_ref[...] = jnp.zeros(o_ref.shape, F32)

    o_ref[...] += jnp.dot(h_ref[kt], w_ref[...], preferred_element_type=F32)


def _in_proj_seq(x, shift, scale, g_pre, w_in_bf16, conv_w8, conv_b, g_cn, b_cn, dc, da, q_scale,
                 bm=256, conv_bt=128):
    m, d = x.shape
    n = w_in_bf16.shape[1]
    assert bm % conv_bt == 0 and shift.shape[0] == 1
    kern = functools.partial(_proj_seq_kernel, dc=dc, da=da, q_scale=q_scale)
    row = lambda w: pl.BlockSpec((bm, w), lambda i: (i, 0))
    const = lambda a: pl.BlockSpec(a.shape, lambda i: (0,) * a.ndim)
    return pl.pallas_call(
        kern,
        out_shape=(jax.ShapeDtypeStruct((m, dc), BF16), jax.ShapeDtypeStruct((CONV_HALO, dc), F32),
                   jax.ShapeDtypeStruct((m, da), BF16), jax.ShapeDtypeStruct((m, da), F32),
                   jax.ShapeDtypeStruct((m, da), F32), jax.ShapeDtypeStruct((m, da), BF16)),
        grid=(m // bm,),
        in_specs=[row(d), const(shift), const(scale), const(g_pre),
                  pl.BlockSpec((d, n), lambda i: (0, 0), pipeline_mode=pl.Buffered(1)),
                  const(conv_w8), const(conv_b), const(g_cn), const(b_cn)],
        out_specs=(row(dc), pl.BlockSpec((CONV_HALO, dc), lambda i: (0, 0)),
                   row(da), row(da), row(da), row(da)),
        scratch_shapes=[pltpu.VMEM((bm + CONV_HALO, dc), F32),
                        pltpu.VMEM((SUBLANES - 1, conv_bt + CONV_HALO - SUBLANES, dc), F32),
                        pltpu.VMEM((bm, dc), F32)],
        compiler_params=_params("arbitrary"),
        name="in_proj_seq",
    )(x, shift, scale, g_pre, w_in_bf16, conv_w8, conv_b, g_cn, b_cn)


def _in_proj_rows(x, shift, scale, g_pre, w_in_bf16, bk=MXU_TILE):
    m, d = x.shape
    n = w_in_bf16.shape[1]
    const = lambda a: pl.BlockSpec(a.shape, lambda j: (0,) * a.ndim)
    return pl.pallas_call(
        _proj_rows_kernel,
        out_shape=jax.ShapeDtypeStruct((m, n), F32),
        grid=(d // bk,),
        in_specs=[const(x), const(shift), const(scale), const(g_pre), pl.BlockSpec((bk, n), lambda j: (j, 0))],
        out_specs=pl.BlockSpec((m, n), lambda j: (0, 0)),
        scratch_shapes=[pltpu.VMEM((d // bk, m, bk), BF16)],
        compiler_params=_params("arbitrary"),
        name="in_proj_rows",
    )(x, shift, scale, g_pre, w_in_bf16)


def _conv_sample_kernel(state_ref, a_ref, b_ref, z_ref, w_ref, cb_ref, gcn_ref, bcn_ref,
                        o_ref, ns_ref, conv_ref):
    nb = state_ref.shape[0]
    w_hist = w_ref[0:CONV_STATE, :]
    u = a_ref[...] * _sigmoid(b_ref[...])
    for b in range(nb):
        st = state_ref[b]
        conv_ref[b:b + 1, :] = jnp.sum(st * w_hist, axis=0, keepdims=True)
        ns_ref[b, 0:CONV_STATE - 1, :] = state_ref[b, 1:CONV_STATE, :]
        ns_ref[b, CONV_STATE - 1:CONV_STATE, :] = u[b:b + 1, :]
    conv = conv_ref[...] + w_ref[CONV_STATE:CONV_WIDTH, :] * u + cb_ref[...]
    o_ref[...] = _ln_swish_gate(conv, gcn_ref, bcn_ref, _silu(z_ref[...])).astype(o_ref.dtype)


def _conv_sample(state, proj, conv_w, conv_b, g_cn, b_cn):
    nb, _, dc = state.shape
    const = lambda a: pl.BlockSpec(a.shape, lambda i: (0,) * a.ndim)
    col = lambda c: pl.BlockSpec((nb, dc), lambda i: (0, c))
    return pl.pallas_call(
        _conv_sample_kernel,
        out_shape=(jax.ShapeDtypeStruct((nb, dc), BF16),
                   jax.ShapeDtypeStruct((nb, CONV_STATE, dc), F32)),
        grid=(1,),
        in_specs=[const(state), col(0), col(1), col(2), const(conv_w), const(conv_b), const(g_cn), const(b_cn)],
        out_specs=(pl.BlockSpec((nb, dc), lambda i: (0, 0)),
                   pl.BlockSpec((nb, CONV_STATE, dc), lambda i: (0, 0, 0))),
        scratch_shapes=[pltpu.VMEM((nb, dc), F32)],
        compiler_params=_params("arbitrary"),
        name="conv_sample",
    )(state, proj, proj, proj, conv_w, conv_b, g_cn, b_cn)


def _lambda_full(lq1_ref, lk1_ref, lq2_ref, lk2_ref, layer_idx):
    s1 = jnp.sum(lq1_ref[...] * lk1_ref[...], axis=-1, keepdims=True)
    s2 = jnp.sum(lq2_ref[...] * lk2_ref[...], axis=-1, keepdims=True)
    return jnp.exp(s1) - jnp.exp(s2) + _lambda_init(layer_idx)


def _subln_gate(o, g_ref, gate, layer_idx):
    y = o * lax.rsqrt(jnp.mean(o * o, axis=-1, keepdims=True) + EPS) * g_ref[...]
    return y * (1.0 - _lambda_init(layer_idx)) * gate


def _trips_before(i):
    return (i * i) // 4


def _attn_kernel(pt_ref,
                 q_ref, k_ref, v_ref, ga_ref, g_ref, lq1_ref, lk1_ref, lq2_ref, lk2_ref,
                 qd_ref, knd_ref, vnd_ref, zd_ref, ck_hbm, cv_hbm,
                 o_ref, od_ref,
                 kb_ref, vt_ref, qs_ref, sa_ref, sb_ref, mxa_ref, mxb_ref, m_ref, acc_ref,
                 kpg_ref, vpg_ref, ksem, vsem, md_ref, ld_ref, accd_ref,
                 *, bq, layer_idx, pages, trips_per_seq, n_trips):
    h = pl.program_id(0)
    i = pl.program_id(1)
    n_chunks = kb_ref.shape[0]
    hd = q_ref.shape[1]
    half = hd // 2
    nh = qd_ref.shape[1]
    n_slots = kpg_ref.shape[0]
    trips_per_head = _trips_before(n_chunks)

    def page_copies(trip, slot, from_table):
        copies = []
        for j in range(pages):
            page = pt_ref[trip * pages + j] if from_table else 0
            copies.append(pltpu.make_async_copy(ck_hbm.at[page], kpg_ref.at[slot, j], ksem.at[slot]))
            copies.append(pltpu.make_async_copy(cv_hbm.at[page], vpg_ref.at[slot, j], vsem.at[slot]))
        return copies

    @pl.when(jnp.logical_and(h == 0, i == 0))
    def _():
        for trip in range(n_slots - 1):
            for cp in page_copies(trip, trip, True):
                cp.start()

    ones_rows = jnp.where(lax.broadcasted_iota(jnp.int32, (ONES_ROWS, bq), 0) == 0, 1.0, 0.0).astype(BF16)

    @pl.when(i == 0)
    def _():
        for c in range(n_chunks):
            kb_ref[c] = k_ref[c * bq:(c + 1) * bq, :].astype(BF16)
            vt_ref[c, 0:hd, :] = v_ref[c * bq:(c + 1) * bq, :].T.astype(BF16)
            vt_ref[c, hd:hd + ONES_ROWS, :] = ones_rows

    q_t = q_ref[...].astype(F32).T
    d_idx = lax.broadcasted_iota(jnp.int32, q_t.shape, 0)
    qs_ref[:, 0:bq] = jnp.where(d_idx < half, q_t, 0.0).astype(BF16)
    qs_ref[:, bq:2 * bq] = jnp.where(d_idx >= half, q_t, 0.0).astype(BF16)
    m_ref[...] = jnp.full(m_ref.shape, -jnp.inf, F32)
    acc_ref[...] = jnp.zeros(acc_ref.shape, F32)

    tile_w = MXU_TILE
    n_tiles = 2 * bq // tile_w

    def scores_tile(c, s_ref, mx_ref, t, diagonal=False):
        cs = slice(t * tile_w, (t + 1) * tile_w)
        s = jnp.dot(kb_ref[c], qs_ref[:, cs], preferred_element_type=F32)
        if diagonal:
            kidx = lax.broadcasted_iota(jnp.int32, s.shape, 0)
            qidx = (lax.broadcasted_iota(jnp.int32, s.shape, 1) + t * tile_w) & (bq - 1)
            s = jnp.where(kidx <= qidx, s, -jnp.inf)
        s_ref[:, cs] = s
        mx_ref[:, cs] = jnp.max(s, axis=0, keepdims=True)

    def accumulate_tile(c, s_ref, mx_ref, t):
        cs = slice(t * tile_w, (t + 1) * tile_w)
        m_old = m_ref[:, cs]
        m_new = jnp.maximum(m_old, mx_ref[:, cs])
        p = jnp.exp2(s_ref[:, cs] - m_new)
        alpha = jnp.exp2(m_old - m_new)
        pv = jnp.dot(vt_ref[c], p.astype(BF16), preferred_element_type=F32)
        acc_ref[:, cs] = alpha * acc_ref[:, cs] + pv
        m_ref[:, cs] = m_new

    def decode_query(b):
        q = qd_ref[b] * half ** -0.5
        lane = lax.broadcasted_iota(jnp.int32, q.shape, 1)
        qx = jnp.concatenate([jnp.where(lane < half, q, 0.0), jnp.where(lane >= half, q, 0.0)], axis=0)
        return qx.astype(BF16)

    def decode_scores(b, slot):
        rows = pages * kpg_ref.shape[2]
        col_head = lax.broadcasted_iota(jnp.int32, (2 * nh, rows), 1) & (nh - 1)
        row_head = lax.broadcasted_iota(jnp.int32, (2 * nh, rows), 0) & (nh - 1)
        k_all = kpg_ref[slot].reshape(rows, hd).astype(BF16)
        s = lax.dot_general(decode_query(b), k_all, NT_DIMS, preferred_element_type=F32)
        return jnp.where(col_head == row_head, s, -jnp.inf)

    def decode_update(s, slot, first):
        m_old = jnp.where(first, -jnp.inf, md_ref[...])
        l_old = jnp.where(first, 0.0, ld_ref[...])
        acc_old = jnp.where(first, 0.0, accd_ref[...])
        m_new = jnp.maximum(m_old, jnp.max(s, axis=-1, keepdims=True))
        alpha = jnp.exp(m_old - m_new)
        p = jnp.exp(s - m_new)
        v_all = vpg_ref[slot].reshape(s.shape[1], hd).astype(BF16)
        md_ref[...] = m_new
        ld_ref[...] = alpha * l_old + jnp.sum(p, axis=-1, keepdims=True)
        accd_ref[...] = alpha * acc_old + jnp.dot(p.astype(BF16), v_all, preferred_element_type=F32)

    def decode_finish(b, lam):
        qx = decode_query(b).astype(F32)
        kn = knd_ref[b].astype(BF16).astype(F32)
        vn = vnd_ref[b].astype(BF16).astype(F32)
        kn2 = jnp.concatenate([kn, kn], axis=0)
        vn2 = jnp.concatenate([vn, vn], axis=0)
        s_new = jnp.sum(qx * kn2, axis=-1, keepdims=True)
        m_old = md_ref[...]
        m_fin = jnp.maximum(m_old, s_new)
        a_fin = jnp.exp(m_old - m_fin)
        p_new = jnp.exp(s_new - m_fin)
        l_fin = a_fin * ld_ref[...] + p_new
        on = (a_fin * accd_ref[...] + p_new * vn2) / l_fin
        o = on[0:nh] - lam * on[nh:2 * nh]
        od_ref[b] = _subln_gate(o, g_ref, _silu(zd_ref[b]), layer_idx)

    for t in range(n_tiles):
        scores_tile(i, sa_ref, mxa_ref, t, diagonal=True)
    trip0 = h * trips_per_head + _trips_before(i)

    def body(p, carry):
        trip = trip0 + p
        slot = lax.rem(trip, n_slots)
        for cp in page_copies(trip, slot, False):
            cp.wait()

        @pl.when(trip + n_slots - 1 < n_trips)
        def _():
            nxt = trip + n_slots - 1
            for cp in page_copies(nxt, lax.rem(nxt, n_slots), True):
                cp.start()

        seq = trip // trips_per_seq
        part = lax.rem(trip, trips_per_seq)
        c_cur = jnp.where(p == 0, i, 2 * p - 1)
        for t in range(n_tiles):
            scores_tile(2 * p, sb_ref, mxb_ref, t)
            accumulate_tile(c_cur, sa_ref, mxa_ref, t)
        s_dec = decode_scores(seq, slot)
        c_next = jnp.minimum(2 * p + 1, i - 1)
        for t in range(n_tiles):
            scores_tile(c_next, sa_ref, mxa_ref, t)
            accumulate_tile(2 * p, sb_ref, mxb_ref, t)
        decode_update(s_dec, slot, part == 0)

        @pl.when(part == trips_per_seq - 1)
        def _():
            decode_finish(seq, _lambda_full(lq1_ref, lk1_ref, lq2_ref, lk2_ref, layer_idx))

        return carry

    lax.fori_loop(0, (i + 1) // 2, body, 0)

    @pl.when(i % 2 == 0)
    def _():
        for t in range(n_tiles):
            accumulate_tile(jnp.maximum(i - 1, 0), sa_ref, mxa_ref, t)

    lam = _lambda_full(lq1_ref, lk1_ref, lq2_ref, lk2_ref, layer_idx)
    acc = acc_ref[0:hd, :]
    inv_l = 1.0 / acc_ref[hd:hd + 1, :]
    o_t = acc[:, 0:bq] * inv_l[:, 0:bq] - lam * (acc[:, bq:2 * bq] * inv_l[:, bq:2 * bq])
    o = o_t.T
    o_ref[...] = _subln_gate(o, g_ref, ga_ref[...].astype(F32), layer_idx).astype(o_ref.dtype)


def _attention(q, k, v, ga, g_subln, lq1, lk1, lq2, lk2,
               page_table, proj_d, att_col, cache_k, cache_v, layer_idx, bq=512, n_slots=3):
    s, da = q.shape
    hd = da // N_HEADS
    nq = s // bq
    nb, n_pages = page_table.shape
    n_pool, page, nh, _ = cache_k.shape
    ck = cache_k.reshape(n_pool, page * nh, hd)
    cv = cache_v.reshape(n_pool, page * nh, hd)
    pt = page_table.reshape(nb * n_pages)
    assert att_col % da == 0
    proj_heads = proj_d.reshape(nb, proj_d.shape[1] // hd, hd)
    first_blk = att_col // da

    n_trips = N_HEADS * _trips_before(nq)
    pages = (nb * n_pages) // n_trips
    assert pages * n_trips == nb * n_pages and n_pages % pages == 0, (n_trips, nb, n_pages)
    assert n_trips >= n_slots
    trips_per_seq = n_pages // pages

    kern = functools.partial(_attn_kernel, bq=bq, layer_idx=layer_idx, pages=pages,
                             trips_per_seq=trips_per_seq, n_trips=n_trips)
    tile = pl.BlockSpec((bq, hd), lambda h, i, pt: (i, h))
    full = pl.BlockSpec((s, hd), lambda h, i, pt: (0, h))
    small = lambda a: pl.BlockSpec(a.shape, lambda h, i, pt: (0, 0))
    whole = pl.BlockSpec((nb, nh, hd), lambda h, i, pt: (0, 0, 0))
    dec = lambda blk: pl.BlockSpec((nb, nh, hd), lambda h, i, pt: (0, first_blk + blk, 0))
    hbm = pl.BlockSpec(memory_space=pl.ANY)
    grid_spec = pltpu.PrefetchScalarGridSpec(
        num_scalar_prefetch=1,
        grid=(N_HEADS, nq),
        in_specs=[tile, full, full, tile, small(g_subln), small(lq1), small(lk1), small(lq2), small(lk2),
                  dec(0), dec(1), dec(2), dec(3), hbm, hbm],
        out_specs=(tile, whole),
        scratch_shapes=[pltpu.VMEM((nq, bq, hd), BF16),
                        pltpu.VMEM((nq, hd + ONES_ROWS, bq), BF16),
                        pltpu.VMEM((hd, 2 * bq), BF16),
                        pltpu.VMEM((bq, 2 * bq), F32), pltpu.VMEM((bq, 2 * bq), F32),
                        pltpu.VMEM((1, 2 * bq), F32), pltpu.VMEM((1, 2 * bq), F32),
                        pltpu.VMEM((1, 2 * bq), F32),
                        pltpu.VMEM((hd + ONES_ROWS, 2 * bq), F32),
                        pltpu.VMEM((n_slots, pages, page * nh, hd), F32),
                        pltpu.VMEM((n_slots, pages, page * nh, hd), F32),
                        pltpu.SemaphoreType.DMA((n_slots,)), pltpu.SemaphoreType.DMA((n_slots,)),
                        pltpu.VMEM((2 * nh, 1), F32), pltpu.VMEM((2 * nh, 1), F32),
                        pltpu.VMEM((2 * nh, hd), F32)],
    )
    return pl.pallas_call(
        kern,
        out_shape=(jax.ShapeDtypeStruct((s, da), BF16), jax.ShapeDtypeStruct((nb, nh, hd), F32)),
        grid_spec=grid_spec,
        compiler_params=_params("arbitrary", "arbitrary"),
        name="attention",
    )(pt, q, k, v, ga, g_subln, lq1, lk1, lq2, lk2,
      proj_heads, proj_heads, proj_heads, proj_heads, ck, cv)


def _out_kernel(oc_ref, oa_ref, wc_ref, wa_ref, x_ref, gate_ref, g_ref, o_ref):
    y = jnp.dot(oc_ref[...].astype(BF16), wc_ref[...], preferred_element_type=F32)
    y = y + jnp.dot(oa_ref[...].astype(BF16), wa_ref[...], preferred_element_type=F32)
    yn = y * lax.rsqrt(jnp.mean(y * y, axis=-1, keepdims=True) + EPS) * g_ref[...]
    o_ref[...] = x_ref[...] + gate_ref[...] * yn


def _out_proj(oc, oa, w_out_bf16, x, gate, g_post, bm):
    m, d = x.shape
    dc = oc.shape[1]
    da = oa.shape[1]
    per_row = gate.shape[0] != 1
    assert dc % da == 0
    gate_spec = pl.BlockSpec((bm, d), lambda i: (i, 0)) if per_row else pl.BlockSpec((1, d), lambda i: (0, 0))
    return pl.pallas_call(
        _out_kernel,
        out_shape=jax.ShapeDtypeStruct((m, d), F32),
        grid=(m // bm,),
        in_specs=[pl.BlockSpec((bm, dc), lambda i: (i, 0)),
                  pl.BlockSpec((bm, da), lambda i: (i, 0)),
                  pl.BlockSpec((dc, d), lambda i: (0, 0)),
                  pl.BlockSpec((da, d), lambda i: (dc // da, 0)),
                  pl.BlockSpec((bm, d), lambda i: (i, 0)),
                  gate_spec,
                  pl.BlockSpec((1, d), lambda i: (0, 0))],
        out_specs=pl.BlockSpec((bm, d), lambda i: (i, 0)),
        compiler_params=_params("arbitrary"),
        name="out_proj",
    )(oc, oa, w_out_bf16, w_out_bf16, x, gate, g_post)


def kernel(x_prompt, x_sample, cache_k, cache_v, state_conv, page_table, c_prompt, c_sample,
           w_ada, b_ada, g_pre, g_post, w_in, conv_w, conv_b, g_cn, b_cn,
           lq1, lk1, lq2, lk2, g_subln, w_out):
    depth = w_ada.shape[0]
    bp, seq, d = x_prompt.shape
    nb = x_sample.shape[0]
    assert bp == 1 and x_sample.shape[1] == 1
    dc = conv_w.shape[2]
    da = w_out.shape[1] - dc
    hd = da // N_HEADS
    dk = hd // 2

    xp = x_prompt.reshape(seq, d)
    xs = x_sample.reshape(nb, d)
    pad = (-(bp + nb)) % 8
    c_all = jnp.concatenate([c_prompt, c_sample, jnp.zeros((pad, d), F32)], axis=0)
    row = lambda a: a.reshape(1, -1)

    kp_l, vp_l, cp_l, ks_l, vs_l, cs_l = [], [], [], [], [], []
    for l in range(depth):
        w_in_b = w_in[l].astype(BF16)
        w_out_b = w_out[l].astype(BF16)
        mod = _modulation(c_all, w_ada[l], row(b_ada[l]))
        shift, scale, gate = mod[:, 0:d], mod[:, d:2 * d], mod[:, 2 * d:3 * d]
        vecs = (row(conv_b[l]), row(g_cn[l]), row(b_cn[l]))
        lams = (row(lq1[l]), row(lk1[l]), row(lq2[l]), row(lk2[l]))
        g_sub = row(g_subln[l])
        conv_w8 = jnp.broadcast_to(conv_w[l][:, None, :], (CONV_WIDTH, SUBLANES, dc))

        oc, u_tail, q, k, v, ga = _in_proj_seq(xp, shift[0:1], scale[0:1], row(g_pre[l]), w_in_b, conv_w8, *vecs,
                                               dc=dc, da=da, q_scale=dk ** -0.5 * math.log2(math.e))
        sl = slice(bp, bp + nb)
        proj_s = _in_proj_rows(xs, shift[sl], scale[sl], row(g_pre[l]), w_in_b)
        att_col = 3 * dc
        ksn = proj_s[:, att_col + da:att_col + 2 * da]
        vsn = proj_s[:, att_col + 2 * da:att_col + 3 * da]

        ocs, new_state = _conv_sample(state_conv[l], proj_s, conv_w[l], *vecs)

        oa, oas = _attention(q, k, v, ga, g_sub, *lams, page_table, proj_s, att_col,
                             cache_k[l], cache_v[l], layer_idx=l)

        xp = _out_proj(oc, oa, w_out_b, xp, gate[0:1], row(g_post[l]), bm=512)
        xs = _out_proj(ocs, oas.reshape(nb, da), w_out_b, xs, gate[sl], row(g_post[l]), bm=nb)
        kp_l.append(k.reshape(bp, seq, N_HEADS, hd))
        vp_l.append(v.reshape(bp, seq, N_HEADS, hd))
        cp_l.append(u_tail[CONV_HALO - CONV_STATE:].reshape(bp, CONV_STATE, dc))
        ks_l.append(ksn.reshape(nb, 1, N_HEADS, hd))
        vs_l.append(vsn.reshape(nb, 1, N_HEADS, hd))
        cs_l.append(new_state)

    return (xp.reshape(bp, seq, d), xs.reshape(nb, 1, d),
            jnp.stack(kp_l), jnp.stack(vp_l), jnp.stack(cp_l),
            jnp.stack(ks_l), jnp.stack(vs_l), jnp.stack(cs_l))
```

```python
import functools
import math

import jax
import jax.numpy as jnp
from jax import lax
from jax.experimental import pallas as pl
from jax.experimental.pallas import tpu as pltpu

F32 = jnp.float32
BF16 = jnp.bfloat16

N_HEADS = 8
CONV_WIDTH = 31
CONV_STATE = CONV_WIDTH - 1
CONV_HALO = 32
EPS = 1e-6

LANES = 128
SUBLANES = 8
MXU_TILE = 256
VMEM_LIMIT = 56 * 1024 * 1024

NT_DIMS = (((1,), (1,)), ((), ()))


def _lambda_init(layer_idx):
    return 0.8 - 0.6 * math.exp(-0.3 * layer_idx)


def _params(*sem):
    return pltpu.CompilerParams(dimension_semantics=sem, vmem_limit_bytes=VMEM_LIMIT)


def _sigmoid(x):
    return 1.0 / (1.0 + jnp.exp(-x))


def _silu(x):
    return x * _sigmoid(x)


def _mod_kernel(c_ref, w_ref, b_ref, o_ref):
    a = _silu(c_ref[...]).astype(BF16)
    o_ref[...] = jnp.dot(a, w_ref[...].astype(BF16), preferred_element_type=F32) + b_ref[...]


def _modulation(c_all, w_ada, b_ada, bn=512):
    rows, d = c_all.shape
    n = w_ada.shape[1]
    return pl.pallas_call(
        _mod_kernel,
        out_shape=jax.ShapeDtypeStruct((rows, n), F32),
        grid=(n // bn,),
        in_specs=[pl.BlockSpec((rows, d), lambda j: (0, 0)),
                  pl.BlockSpec((d, bn), lambda j: (0, j)),
                  pl.BlockSpec((1, bn), lambda j: (0, j))],
        out_specs=pl.BlockSpec((rows, bn), lambda j: (0, j)),
        compiler_params=_params("arbitrary"),
        name="adaln_mod",
    )(c_all, w_ada, b_ada)


def _ln_swish_gate(conv, g_ref, b_ref, gate):
    mu = jnp.mean(conv, axis=-1, keepdims=True)
    d = conv - mu
    var = jnp.mean(d * d, axis=-1, keepdims=True)
    y = d * lax.rsqrt(var + EPS) * g_ref[...] + b_ref[...]
    return _silu(y) * gate


def _causal_conv_rows(ext_ref, sh_ref, conv_ref, w_ref, cb_ref, row0, bt, rows=32):
    span = sh_ref.shape[1]
    for b in range(1, SUBLANES):
        sh_ref[b - 1] = ext_ref[row0 + b:row0 + b + span, :]
    off = CONV_HALO - CONV_STATE
    reps = rows // SUBLANES
    for c in range(ext_ref.shape[1] // LANES):
        cs = slice(c * LANES, (c + 1) * LANES)
        taps = [jnp.concatenate([w_ref[j, :, cs]] * reps, axis=0) for j in range(CONV_WIDTH)]
        bias = jnp.broadcast_to(cb_ref[:, cs], (rows, LANES))
        for r in range(bt // rows):
            acc = bias
            for j in range(CONV_WIDTH):
                a, b = divmod(off + j, SUBLANES)
                base = r * rows + SUBLANES * a
                if b == 0:
                    src = ext_ref[row0 + base:row0 + base + rows, cs]
                else:
                    src = sh_ref[b - 1, base:base + rows, cs]
                acc = acc + taps[j] * src
            conv_ref[row0 + r * rows:row0 + (r + 1) * rows, cs] = acc


def _modulated_norm(x_ref, shift_ref, scale_ref, g_ref):
    x = x_ref[...]
    y = x * lax.rsqrt(jnp.mean(x * x, axis=-1, keepdims=True) + EPS) * g_ref[...]
    return (y * (1.0 + scale_ref[...]) + shift_ref[...]).astype(BF16)


def _proj_seq_kernel(x_ref, shift_ref, scale_ref, g_ref,
                     wa_ref, wb_ref, wzc_ref, wq_ref, wk_ref, wv_ref, wza_ref,
                     cw_ref, cb_ref, gcn_ref, bcn_ref,
                     oc_ref, utail_ref, q_ref, k_ref, v_ref, ga_ref, ext_ref, sh_ref, conv_ref,
                     *, q_scale):
    h = _modulated_norm(x_ref, shift_ref, scale_ref, g_ref)
    bm = h.shape[0]
    dc = oc_ref.shape[1]

    def seg(w_ref):
        return jnp.dot(h, w_ref[...], preferred_element_type=F32)

    u = seg(wa_ref) * _sigmoid(seg(wb_ref))
    gate_c = _silu(seg(wzc_ref))

    @pl.when(pl.program_id(0) == 0)
    def _():
        ext_ref[0:CONV_HALO, :] = jnp.zeros((CONV_HALO, dc), F32)

    ext_ref[CONV_HALO:CONV_HALO + bm, :] = u
    utail_ref[...] = u[bm - CONV_HALO:bm, :]
    bt = sh_ref.shape[1] + SUBLANES - CONV_HALO
    for row0 in range(0, bm, bt):
        _causal_conv_rows(ext_ref, sh_ref, conv_ref, cw_ref, cb_ref, row0, bt)
    oc_ref[...] = _ln_swish_gate(conv_ref[...], gcn_ref, bcn_ref, gate_c).astype(oc_ref.dtype)
    ext_ref[0:CONV_HALO, :] = ext_ref[bm:bm + CONV_HALO, :]

    q_ref[...] = (seg(wq_ref) * q_scale).astype(q_ref.dtype)
    k_ref[...] = seg(wk_ref)
    v_ref[...] = seg(wv_ref)
    ga_ref[...] = _silu(seg(wza_ref)).astype(ga_ref.dtype)


def _proj_rows_kernel(x_ref, shift_ref, scale_ref, g_ref, *rest):
    *w_refs, o_ref, h_ref = rest
    kt = pl.program_id(0)
    n_kt, _, bk = h_ref.shape

    @pl.when(kt == 0)
    def _():
        h = _modulated_norm(x_ref, shift_ref, scale_ref, g_ref)
        for c in range(n_kt):
            h_ref[c] = h[:, c * bk:(c + 1) * bk]
        o_ref[...] = jnp.zeros(o_ref.shape, F32)

    h = h_ref[kt]
    start = 0
    for w_ref in w_refs:
        width = w_ref.shape[1]
        o_ref[:, start:start + width] += jnp.dot(h, w_ref[...], preferred_element_type=F32)
        start += width


def _in_proj_seq(x, shift, scale, g_pre, w_in_bf16, conv_w8, conv_b, g_cn, b_cn, dc, da, q_scale,
                 bm=256, conv_bt=128):
    m, d = x.shape
    assert bm % conv_bt == 0 and shift.shape[0] == 1
    kern = functools.partial(_proj_seq_kernel, q_scale=q_scale)
    row = lambda w: pl.BlockSpec((bm, w), lambda i: (i, 0))
    const = lambda a: pl.BlockSpec(a.shape, lambda i: (0,) * a.ndim)
    segments = [(t * dc, dc) for t in range(3)] + [(3 * dc + t * da, da) for t in range(4)]
    assert all(start % width == 0 for start, width in segments) and segments[-1][0] + da == w_in_bf16.shape[1]
    w_specs = [pl.BlockSpec((d, width), lambda i, blk=start // width: (0, blk), pipeline_mode=pl.Buffered(1))
               for start, width in segments]
    return pl.pallas_call(
        kern,
        out_shape=(jax.ShapeDtypeStruct((m, dc), BF16), jax.ShapeDtypeStruct((CONV_HALO, dc), F32),
                   jax.ShapeDtypeStruct((m, da), BF16), jax.ShapeDtypeStruct((m, da), F32),
                   jax.ShapeDtypeStruct((m, da), F32), jax.ShapeDtypeStruct((m, da), BF16)),
        grid=(m // bm,),
        in_specs=[row(d), const(shift), const(scale), const(g_pre), *w_specs,
                  const(conv_w8), const(conv_b), const(g_cn), const(b_cn)],
        out_specs=(row(dc), pl.BlockSpec((CONV_HALO, dc), lambda i: (0, 0)),
                   row(da), row(da), row(da), row(da)),
        scratch_shapes=[pltpu.VMEM((bm + CONV_HALO, dc), F32),
                        pltpu.VMEM((SUBLANES - 1, conv_bt + CONV_HALO - SUBLANES, dc), F32),
                        pltpu.VMEM((bm, dc), F32)],
        compiler_params=_params("arbitrary"),
        name="in_proj_seq",
    )(x, shift, scale, g_pre, *([w_in_bf16] * len(segments)), conv_w8, conv_b, g_cn, b_cn)


def _in_proj_rows(x, shift, scale, g_pre, w_in_bf16, bk=MXU_TILE, parts=4):
    m, d = x.shape
    n = w_in_bf16.shape[1]
    assert n % (parts * LANES) == 0
    const = lambda a: pl.BlockSpec(a.shape, lambda j: (0,) * a.ndim)
    w_specs = [pl.BlockSpec((bk, n // parts), lambda j, c=c: (j, c)) for c in range(parts)]
    return pl.pallas_call(
        _proj_rows_kernel,
        out_shape=jax.ShapeDtypeStruct((m, n), F32),
        grid=(d // bk,),
        in_specs=[const(x), const(shift), const(scale), const(g_pre), *w_specs],
        out_specs=pl.BlockSpec((m, n), lambda j: (0, 0)),
        scratch_shapes=[pltpu.VMEM((d // bk, m, bk), BF16)],
        compiler_params=_params("arbitrary"),
        name="in_proj_rows",
    )(x, shift, scale, g_pre, *([w_in_bf16] * parts))


def _conv_sample_kernel(state_ref, a_ref, b_ref, z_ref, w_ref, cb_ref, gcn_ref, bcn_ref,
                        o_ref, ns_ref, conv_ref):
    nb = state_ref.shape[0]
    w_hist = w_ref[0:CONV_STATE, :]
    u = a_ref[...] * _sigmoid(b_ref[...])
    for b in range(nb):
        st = state_ref[b]
        conv_ref[b:b + 1, :] = jnp.sum(st * w_hist, axis=0, keepdims=True)
        ns_ref[b, 0:CONV_STATE - 1, :] = state_ref[b, 1:CONV_STATE, :]
        ns_ref[b, CONV_STATE - 1:CONV_STATE, :] = u[b:b + 1, :]
    conv = conv_ref[...] + w_ref[CONV_STATE:CONV_WIDTH, :] * u + cb_ref[...]
    o_ref[...] = _ln_swish_gate(conv, gcn_ref, bcn_ref, _silu(z_ref[...])).astype(o_ref.dtype)


def _conv_sample(state, proj, conv_w, conv_b, g_cn, b_cn):
    nb, _, dc = state.shape
    const = lambda a: pl.BlockSpec(a.shape, lambda i: (0,) * a.ndim)
    col = lambda c: pl.BlockSpec((nb, dc), lambda i: (0, c))
    return pl.pallas_call(
        _conv_sample_kernel,
        out_shape=(jax.ShapeDtypeStruct((nb, dc), BF16),
                   jax.ShapeDtypeStruct((nb, CONV_STATE, dc), F32)),
        grid=(1,),
        in_specs=[const(state), col(0), col(1), col(2), const(conv_w), const(conv_b), const(g_cn), const(b_cn)],
        out_specs=(pl.BlockSpec((nb, dc), lambda i: (0, 0)),
                   pl.BlockSpec((nb, CONV_STATE, dc), lambda i: (0, 0, 0))),
        scratch_shapes=[pltpu.VMEM((nb, dc), F32)],
        compiler_params=_params("arbitrary"),
        name="conv_sample",
    )(state, proj, proj, proj, conv_w, conv_b, g_cn, b_cn)


def _lambda_full(lq1_ref, lk1_ref, lq2_ref, lk2_ref, layer_idx):
    s1 = jnp.sum(lq1_ref[...] * lk1_ref[...], axis=-1, keepdims=True)
    s2 = jnp.sum(lq2_ref[...] * lk2_ref[...], axis=-1, keepdims=True)
    return jnp.exp(s1) - jnp.exp(s2) + _lambda_init(layer_idx)


def _subln_gate(o, g_ref, gate, layer_idx):
    y = o * lax.rsqrt(jnp.mean(o * o, axis=-1, keepdims=True) + EPS) * g_ref[...]
    return y * (1.0 - _lambda_init(layer_idx)) * gate


def _trips_before(i):
    return (i * i) // 4


def _attn_kernel(pt_ref,
                 q_ref, k_ref, v_ref, ga_ref, g_ref, lq1_ref, lk1_ref, lq2_ref, lk2_ref,
                 qd_ref, knd_ref, vnd_ref, zd_ref, ck_hbm, cv_hbm,
                 o_ref, od_ref,
                 kb_ref, vt_ref, qs_ref, sa_ref, sb_ref, mxa_ref, mxb_ref, m_ref, l_ref, acc_ref,
                 kpg_ref, vpg_ref, ksem, vsem, md_ref, ld_ref, accd_ref,
                 *, bq, layer_idx, pages, trips_per_seq, n_trips):
    h = pl.program_id(0)
    i = pl.program_id(1)
    n_chunks = kb_ref.shape[0]
    hd = q_ref.shape[1]
    half = hd // 2
    nh = qd_ref.shape[1]
    n_slots = kpg_ref.shape[0]
    trips_per_head = _trips_before(n_chunks)

    def page_copies(trip, slot, from_table):
        copies = []
        for j in range(pages):
            page = pt_ref[trip * pages + j] if from_table else 0
            copies.append(pltpu.make_async_copy(ck_hbm.at[page], kpg_ref.at[slot, j], ksem.at[slot]))
            copies.append(pltpu.make_async_copy(cv_hbm.at[page], vpg_ref.at[slot, j], vsem.at[slot]))
        return copies

    @pl.when(jnp.logical_and(h == 0, i == 0))
    def _():
        for trip in range(n_slots - 1):
            for cp in page_copies(trip, trip, True):
                cp.start()

    @pl.when(i == 0)
    def _():
        for c in range(n_chunks):
            kb_ref[c] = k_ref[c * bq:(c + 1) * bq, :].astype(BF16)
            vt_ref[c] = v_ref[c * bq:(c + 1) * bq, :].T.astype(BF16)

    q_t = q_ref[...].astype(F32).T
    d_idx = lax.broadcasted_iota(jnp.int32, q_t.shape, 0)
    qs_ref[:, 0:bq] = jnp.where(d_idx < half, q_t, 0.0).astype(BF16)
    qs_ref[:, bq:2 * bq] = jnp.where(d_idx >= half, q_t, 0.0).astype(BF16)
    m_ref[...] = jnp.full(m_ref.shape, -jnp.inf, F32)
    l_ref[...] = jnp.zeros(l_ref.shape, F32)
    acc_ref[...] = jnp.zeros(acc_ref.shape, F32)

    tile_w = MXU_TILE
    n_tiles = 2 * bq // tile_w

    def scores_tile(c, s_ref, mx_ref, t, diagonal=False):
        cs = slice(t * tile_w, (t + 1) * tile_w)
        s = jnp.dot(kb_ref[c], qs_ref[:, cs], preferred_element_type=F32)
        if diagonal:
            kidx = lax.broadcasted_iota(jnp.int32, s.shape, 0)
            qidx = (lax.broadcasted_iota(jnp.int32, s.shape, 1) + t * tile_w) & (bq - 1)
            s = jnp.where(kidx <= qidx, s, -jnp.inf)
        s_ref[:, cs] = s
        mx_ref[:, cs] = jnp.max(s, axis=0, keepdims=True)

    def accumulate_tile(c, s_ref, mx_ref, t):
        cs = slice(t * tile_w, (t + 1) * tile_w)
        m_old = m_ref[:, cs]
        m_new = jnp.maximum(m_old, mx_ref[:, cs])
        p = jnp.exp2(s_ref[:, cs] - m_new)
        alpha = jnp.exp2(m_old - m_new)
        l_ref[:, cs] = alpha * l_ref[:, cs] + jnp.sum(p, axis=0, keepdims=True)
        pv = jnp.dot(vt_ref[c], p.astype(BF16), preferred_element_type=F32)
        acc_ref[:, cs] = alpha * acc_ref[:, cs] + pv
        m_ref[:, cs] = m_new

    def decode_query(b):
        q = qd_ref[b] * half ** -0.5
        lane = lax.broadcasted_iota(jnp.int32, q.shape, 1)
        qx = jnp.concatenate([jnp.where(lane < half, q, 0.0), jnp.where(lane >= half, q, 0.0)], axis=0)
        return qx.astype(BF16)

    def decode_scores(b, slot):
        rows = pages * kpg_ref.shape[2]
        col_head = lax.broadcasted_iota(jnp.int32, (2 * nh, rows), 1) & (nh - 1)
        row_head = lax.broadcasted_iota(jnp.int32, (2 * nh, rows), 0) & (nh - 1)
        k_all = kpg_ref[slot].reshape(rows, hd).astype(BF16)
        s = lax.dot_general(decode_query(b), k_all, NT_DIMS, preferred_element_type=F32)
        return jnp.where(col_head == row_head, s, -jnp.inf)

    def decode_update(s, slot, first):
        m_old = jnp.where(first, -jnp.inf, md_ref[...])
        l_old = jnp.where(first, 0.0, ld_ref[...])
        acc_old = jnp.where(first, 0.0, accd_ref[...])
        m_new = jnp.maximum(m_old, jnp.max(s, axis=-1, keepdims=True))
        alpha = jnp.exp(m_old - m_new)
        p = jnp.exp(s - m_new)
        v_all = vpg_ref[slot].reshape(s.shape[1], hd).astype(BF16)
        md_ref[...] = m_new
        ld_ref[...] = alpha * l_old + jnp.sum(p, axis=-1, keepdims=True)
        accd_ref[...] = alpha * acc_old + jnp.dot(p.astype(BF16), v_all, preferred_element_type=F32)

    def decode_finish(b, lam):
        qx = decode_query(b).astype(F32)
        kn = knd_ref[b].astype(BF16).astype(F32)
        vn = vnd_ref[b].astype(BF16).astype(F32)
        kn2 = jnp.concatenate([kn, kn], axis=0)
        vn2 = jnp.concatenate([vn, vn], axis=0)
        s_new = jnp.sum(qx * kn2, axis=-1, keepdims=True)
        m_old = md_ref[...]
        m_fin = jnp.maximum(m_old, s_new)
        a_fin = jnp.exp(m_old - m_fin)
        p_new = jnp.exp(s_new - m_fin)
        l_fin = a_fin * ld_ref[...] + p_new
        on = (a_fin * accd_ref[...] + p_new * vn2) / l_fin
        o = on[0:nh] - lam * on[nh:2 * nh]
        od_ref[b] = _subln_gate(o, g_ref, _silu(zd_ref[b]), layer_idx)

    for t in range(n_tiles):
        scores_tile(i, sa_ref, mxa_ref, t, diagonal=True)
    trip0 = h * trips_per_head + _trips_before(i)

    def body(p, carry):
        trip = trip0 + p
        slot = lax.rem(trip, n_slots)
        for cp in page_copies(trip, slot, False):
            cp.wait()

        @pl.when(trip + n_slots - 1 < n_trips)
        def _():
            nxt = trip + n_slots - 1
            for cp in page_copies(nxt, lax.rem(nxt, n_slots), True):
                cp.start()

        seq = trip // trips_per_seq
        part = lax.rem(trip, trips_per_seq)
        c_cur = jnp.where(p == 0, i, 2 * p - 1)
        for t in range(n_tiles):
            scores_tile(2 * p, sb_ref, mxb_ref, t)
            accumulate_tile(c_cur, sa_ref, mxa_ref, t)
        s_dec = decode_scores(seq, slot)
        c_next = jnp.minimum(2 * p + 1, i - 1)
        for t in range(n_tiles):
            scores_tile(c_next, sa_ref, mxa_ref, t)
            accumulate_tile(2 * p, sb_ref, mxb_ref, t)
        decode_update(s_dec, slot, part == 0)

        @pl.when(part == trips_per_seq - 1)
        def _():
            decode_finish(seq, _lambda_full(lq1_ref, lk1_ref, lq2_ref, lk2_ref, layer_idx))

        return carry

    lax.fori_loop(0, (i + 1) // 2, body, 0)

    @pl.when(i % 2 == 0)
    def _():
        for t in range(n_tiles):
            accumulate_tile(jnp.maximum(i - 1, 0), sa_ref, mxa_ref, t)

    lam = _lambda_full(lq1_ref, lk1_ref, lq2_ref, lk2_ref, layer_idx)
    acc = acc_ref[...]
    inv_l = 1.0 / l_ref[...]
    o_t = acc[:, 0:bq] * inv_l[:, 0:bq] - lam * (acc[:, bq:2 * bq] * inv_l[:, bq:2 * bq])
    o = o_t.T
    o_ref[...] = _subln_gate(o, g_ref, ga_ref[...].astype(F32), layer_idx).astype(o_ref.dtype)


def _attention(q, k, v, ga, g_subln, lq1, lk1, lq2, lk2,
               page_table, proj_d, att_col, cache_k, cache_v, layer_idx, bq=512, n_slots=3):
    s, da = q.shape
    hd = da // N_HEADS
    nq = s // bq
    nb, n_pages = page_table.shape
    n_pool, page, nh, _ = cache_k.shape
    ck = cache_k.reshape(n_pool, page * nh, hd)
    cv = cache_v.reshape(n_pool, page * nh, hd)
    pt = page_table.reshape(nb * n_pages)
    assert att_col % da == 0
    proj_heads = proj_d.reshape(nb, proj_d.shape[1] // hd, hd)
    first_blk = att_col // da

    n_trips = N_HEADS * _trips_before(nq)
    pages = (nb * n_pages) // n_trips
    assert pages * n_trips == nb * n_pages and n_pages % pages == 0, (n_trips, nb, n_pages)
    assert n_trips >= n_slots
    trips_per_seq = n_pages // pages

    kern = functools.partial(_attn_kernel, bq=bq, layer_idx=layer_idx, pages=pages,
                             trips_per_seq=trips_per_seq, n_trips=n_trips)
    tile = pl.BlockSpec((bq, hd), lambda h, i, pt: (i, h))
    full = pl.BlockSpec((s, hd), lambda h, i, pt: (0, h))
    small = lambda a: pl.BlockSpec(a.shape, lambda h, i, pt: (0, 0))
    whole = pl.BlockSpec((nb, nh, hd), lambda h, i, pt: (0, 0, 0))
    dec = lambda blk: pl.BlockSpec((nb, nh, hd), lambda h, i, pt: (0, first_blk + blk, 0))
    hbm = pl.BlockSpec(memory_space=pl.ANY)
    grid_spec = pltpu.PrefetchScalarGridSpec(
        num_scalar_prefetch=1,
        grid=(N_HEADS, nq),
        in_specs=[tile, full, full, tile, small(g_subln), small(lq1), small(lk1), small(lq2), small(lk2),
                  dec(0), dec(1), dec(2), dec(3), hbm, hbm],
        out_specs=(tile, whole),
        scratch_shapes=[pltpu.VMEM((nq, bq, hd), BF16),
                        pltpu.VMEM((nq, hd, bq), BF16),
                        pltpu.VMEM((hd, 2 * bq), BF16),
                        pltpu.VMEM((bq, 2 * bq), F32), pltpu.VMEM((bq, 2 * bq), F32),
                        pltpu.VMEM((1, 2 * bq), F32), pltpu.VMEM((1, 2 * bq), F32),
                        pltpu.VMEM((1, 2 * bq), F32), pltpu.VMEM((1, 2 * bq), F32),
                        pltpu.VMEM((hd, 2 * bq), F32),
                        pltpu.VMEM((n_slots, pages, page * nh, hd), F32),
                        pltpu.VMEM((n_slots, pages, page * nh, hd), F32),
                        pltpu.SemaphoreType.DMA((n_slots,)), pltpu.SemaphoreType.DMA((n_slots,)),
                        pltpu.VMEM((2 * nh, 1), F32), pltpu.VMEM((2 * nh, 1), F32),
                        pltpu.VMEM((2 * nh, hd), F32)],
    )
    return pl.pallas_call(
        kern,
        out_shape=(jax.ShapeDtypeStruct((s, da), BF16), jax.ShapeDtypeStruct((nb, nh, hd), F32)),
        grid_spec=grid_spec,
        compiler_params=_params("arbitrary", "arbitrary"),
        name="attention",
    )(pt, q, k, v, ga, g_subln, lq1, lk1, lq2, lk2,
      proj_heads, proj_heads, proj_heads, proj_heads, ck, cv)


def _out_kernel(oc_ref, oa_ref, wc_ref, wa_ref, x_ref, gate_ref, g_ref, o_ref):
    y = jnp.dot(oc_ref[...].astype(BF16), wc_ref[...], preferred_element_type=F32)
    y = y + jnp.dot(oa_ref[...].astype(BF16), wa_ref[...], preferred_element_type=F32)
    yn = y * lax.rsqrt(jnp.mean(y * y, axis=-1, keepdims=True) + EPS) * g_ref[...]
    o_ref[...] = x_ref[...] + gate_ref[...] * yn


def _out_proj(oc, oa, w_out_bf16, x, gate, g_post, bm):
    m, d = x.shape
    dc = oc.shape[1]
    da = oa.shape[1]
    per_row = gate.shape[0] != 1
    assert dc % da == 0
    gate_spec = pl.BlockSpec((bm, d), lambda i: (i, 0)) if per_row else pl.BlockSpec((1, d), lambda i: (0, 0))
    return pl.pallas_call(
        _out_kernel,
        out_shape=jax.ShapeDtypeStruct((m, d), F32),
        grid=(m // bm,),
        in_specs=[pl.BlockSpec((bm, dc), lambda i: (i, 0)),
                  pl.BlockSpec((bm, da), lambda i: (i, 0)),
                  pl.BlockSpec((dc, d), lambda i: (0, 0)),
                  pl.BlockSpec((da, d), lambda i: (dc // da, 0)),
                  pl.BlockSpec((bm, d), lambda i: (i, 0)),
                  gate_spec,
                  pl.BlockSpec((1, d), lambda i: (0, 0))],
        out_specs=pl.BlockSpec((bm, d), lambda i: (i, 0)),
        compiler_params=_params("arbitrary"),
        name="out_proj",
    )(oc, oa, w_out_bf16, w_out_bf16, x, gate, g_post)


def kernel(x_prompt, x_sample, cache_k, cache_v, state_conv, page_table, c_prompt, c_sample,
           w_ada, b_ada, g_pre, g_post, w_in, conv_w, conv_b, g_cn, b_cn,
           lq1, lk1, lq2, lk2, g_subln, w_out):
    depth = w_ada.shape[0]
    bp, seq, d = x_prompt.shape
    nb = x_sample.shape[0]
    assert bp == 1 and x_sample.shape[1] == 1
    dc = conv_w.shape[2]
    da = w_out.shape[1] - dc
    hd = da // N_HEADS
    dk = hd // 2

    xp = x_prompt.reshape(seq, d)
    xs = x_sample.reshape(nb, d)
    pad = (-(bp + nb)) % 8
    c_all = jnp.concatenate([c_prompt, c_sample, jnp.zeros((pad, d), F32)], axis=0)
    row = lambda a: a.reshape(1, -1)

    kp_l, vp_l, cp_l, ks_l, vs_l, cs_l = [], [], [], [], [], []
    for l in range(depth):
        w_in_b = w_in[l].astype(BF16)
        w_out_b = w_out[l].astype(BF16)
        mod = _modulation(c_all, w_ada[l], row(b_ada[l]))
        shift, scale, gate = mod[:, 0:d], mod[:, d:2 * d], mod[:, 2 * d:3 * d]
        vecs = (row(conv_b[l]), row(g_cn[l]), row(b_cn[l]))
        lams = (row(lq1[l]), row(lk1[l]), row(lq2[l]), row(lk2[l]))
        g_sub = row(g_subln[l])
        conv_w8 = jnp.broadcast_to(conv_w[l][:, None, :], (CONV_WIDTH, SUBLANES, dc))

        oc, u_tail, q, k, v, ga = _in_proj_seq(xp, shift[0:1], scale[0:1], row(g_pre[l]), w_in_b, conv_w8, *vecs,
                                               dc=dc, da=da, q_scale=dk ** -0.5 * math.log2(math.e))
        sl = slice(bp, bp + nb)
        proj_s = _in_proj_rows(xs, shift[sl], scale[sl], row(g_pre[l]), w_in_b)
        att_col = 3 * dc
        ksn = proj_s[:, att_col + da:att_col + 2 * da]
        vsn = proj_s[:, att_col + 2 * da:att_col + 3 * da]

        ocs, new_state = _conv_sample(state_conv[l], proj_s, conv_w[l], *vecs)

        oa, oas = _attention(q, k, v, ga, g_sub, *lams, page_table, proj_s, att_col,
                             cache_k[l], cache_v[l], layer_idx=l)

        xp = _out_proj(oc, oa, w_out_b, xp, gate[0:1], row(g_post[l]), bm=512)
        xs = _out_proj(ocs, oas.reshape(nb, da), w_out_b, xs, gate[sl], row(g_post[l]), bm=nb)
        kp_l.append(k.reshape(bp, seq, N_HEADS, hd))
        vp_l.append(v.reshape(bp, seq, N_HEADS, hd))
        cp_l.append(u_tail[CONV_HALO - CONV_STATE:].reshape(bp, CONV_STATE, dc))
        ks_l.append(ksn.reshape(nb, 1, N_HEADS, hd))
        vs_l.append(vsn.reshape(nb, 1, N_HEADS, hd))
        cs_l.append(new_state)

    return (xp.reshape(bp, seq, d), xs.reshape(nb, 1, d),
            jnp.stack(kp_l), jnp.stack(vp_l), jnp.stack(cp_l),
            jnp.stack(ks_l), jnp.stack(vs_l), jnp.stack(cs_l))
```

```python
import functools
import math

import jax
import jax.numpy as jnp
from jax import lax
from jax.experimental import pallas as pl
from jax.experimental.pallas import tpu as pltpu

F32 = jnp.float32
BF16 = jnp.bfloat16

N_HEADS = 8
CONV_WIDTH = 31
CONV_STATE = CONV_WIDTH - 1
CONV_HALO = 32
EPS = 1e-6

LANES = 128
SUBLANES = 8
MXU_TILE = 256
VMEM_LIMIT = 56 * 1024 * 1024

NT_DIMS = (((1,), (1,)), ((), ()))


def _lambda_init(layer_idx):
    return 0.8 - 0.6 * math.exp(-0.3 * layer_idx)


def _params(*sem):
    return pltpu.CompilerParams(dimension_semantics=sem, vmem_limit_bytes=VMEM_LIMIT)


def _sigmoid(x):
    return 1.0 / (1.0 + jnp.exp(-x))


def _silu(x):
    return x * _sigmoid(x)


def _mod_kernel(c_ref, w_ref, b_ref, o_ref):
    a = _silu(c_ref[...]).astype(BF16)
    o_ref[...] = jnp.dot(a, w_ref[...].astype(BF16), preferred_element_type=F32) + b_ref[...]


def _modulation(c_all, w_ada, b_ada, bn=512):
    rows, d = c_all.shape
    n = w_ada.shape[1]
    return pl.pallas_call(
        _mod_kernel,
        out_shape=jax.ShapeDtypeStruct((rows, n), F32),
        grid=(n // bn,),
        in_specs=[pl.BlockSpec((rows, d), lambda j: (0, 0)),
                  pl.BlockSpec((d, bn), lambda j: (0, j)),
                  pl.BlockSpec((1, bn), lambda j: (0, j))],
        out_specs=pl.BlockSpec((rows, bn), lambda j: (0, j)),
        compiler_params=_params("arbitrary"),
        name="adaln_mod",
    )(c_all, w_ada, b_ada)


def _ln_swish_gate(conv, g_ref, b_ref, gate):
    mu = jnp.mean(conv, axis=-1, keepdims=True)
    d = conv - mu
    var = jnp.mean(d * d, axis=-1, keepdims=True)
    y = d * lax.rsqrt(var + EPS) * g_ref[...] + b_ref[...]
    return _silu(y) * gate


def _causal_conv_rows(ext_ref, sh_ref, conv_ref, w_ref, cb_ref, row0, bt, rows=32):
    span = sh_ref.shape[1]
    for b in range(1, SUBLANES):
        sh_ref[b - 1] = ext_ref[row0 + b:row0 + b + span, :]
    off = CONV_HALO - CONV_STATE
    reps = rows // SUBLANES
    for c in range(ext_ref.shape[1] // LANES):
        cs = slice(c * LANES, (c + 1) * LANES)
        taps = [jnp.concatenate([w_ref[j, :, cs]] * reps, axis=0) for j in range(CONV_WIDTH)]
        bias = jnp.broadcast_to(cb_ref[:, cs], (rows, LANES))
        for r in range(bt // rows):
            acc = bias
            for j in range(CONV_WIDTH):
                a, b = divmod(off + j, SUBLANES)
                base = r * rows + SUBLANES * a
                if b == 0:
                    src = ext_ref[row0 + base:row0 + base + rows, cs]
                else:
                    src = sh_ref[b - 1, base:base + rows, cs]
                acc = acc + taps[j] * src
            conv_ref[row0 + r * rows:row0 + (r + 1) * rows, cs] = acc


def _modulated_norm(x_ref, shift_ref, scale_ref, g_ref):
    x = x_ref[...]
    y = x * lax.rsqrt(jnp.mean(x * x, axis=-1, keepdims=True) + EPS) * g_ref[...]
    return (y * (1.0 + scale_ref[...]) + shift_ref[...]).astype(BF16)


def _proj_kernel(x_ref, shift_ref, scale_ref, xr_ref, shiftr_ref, scaler_ref, g_ref,
                 wa_ref, wb_ref, wzc_ref, wq_ref, wk_ref, wv_ref, wza_ref,
                 cw_ref, cb_ref, gcn_ref, bcn_ref,
                 oc_ref, utail_ref, q_ref, k_ref, v_ref, ga_ref, projr_ref, ext_ref, sh_ref, conv_ref,
                 *, q_scale):
    step = pl.program_id(0)
    last = pl.num_programs(0) - 1
    w_refs = (wa_ref, wb_ref, wzc_ref, wq_ref, wk_ref, wv_ref, wza_ref)

    @pl.when(step < last)
    def _():
        h = _modulated_norm(x_ref, shift_ref, scale_ref, g_ref)
        bm = h.shape[0]
        dc = oc_ref.shape[1]

        def seg(w_ref):
            return jnp.dot(h, w_ref[...], preferred_element_type=F32)

        u = seg(wa_ref) * _sigmoid(seg(wb_ref))
        gate_c = _silu(seg(wzc_ref))

        @pl.when(step == 0)
        def _():
            ext_ref[0:CONV_HALO, :] = jnp.zeros((CONV_HALO, dc), F32)

        ext_ref[CONV_HALO:CONV_HALO + bm, :] = u
        utail_ref[...] = u[bm - CONV_HALO:bm, :]
        bt = sh_ref.shape[1] + SUBLANES - CONV_HALO
        for row0 in range(0, bm, bt):
            _causal_conv_rows(ext_ref, sh_ref, conv_ref, cw_ref, cb_ref, row0, bt)
        oc_ref[...] = _ln_swish_gate(conv_ref[...], gcn_ref, bcn_ref, gate_c).astype(oc_ref.dtype)
        ext_ref[0:CONV_HALO, :] = ext_ref[bm:bm + CONV_HALO, :]

        q_ref[...] = (seg(wq_ref) * q_scale).astype(q_ref.dtype)
        k_ref[...] = seg(wk_ref)
        v_ref[...] = seg(wv_ref)
        ga_ref[...] = _silu(seg(wza_ref)).astype(ga_ref.dtype)

    @pl.when(step == last)
    def _():
        hr = _modulated_norm(xr_ref, shiftr_ref, scaler_ref, g_ref)
        start = 0
        for w_ref in w_refs:
            width = w_ref.shape[1]
            projr_ref[:, start:start + width] = jnp.dot(hr, w_ref[...], preferred_element_type=F32)
            start += width


def _in_proj(x, shift, scale, xr, shift_r, scale_r, g_pre, w_in_bf16, conv_w8, conv_b, g_cn, b_cn,
             dc, da, q_scale, bm=256, conv_bt=128):
    m, d = x.shape
    assert bm % conv_bt == 0 and shift.shape[0] == 1
    n_tiles = m // bm
    kern = functools.partial(_proj_kernel, q_scale=q_scale)
    row = lambda w: pl.BlockSpec((bm, w), lambda i: (jnp.minimum(i, n_tiles - 1), 0))
    const = lambda a: pl.BlockSpec(a.shape, lambda i: (0,) * a.ndim)
    segments = [(t * dc, dc) for t in range(3)] + [(3 * dc + t * da, da) for t in range(4)]
    assert all(start % width == 0 for start, width in segments) and segments[-1][0] + da == w_in_bf16.shape[1]
    w_specs = [pl.BlockSpec((d, width), lambda i, blk=start // width: (0, blk), pipeline_mode=pl.Buffered(1))
               for start, width in segments]
    n = w_in_bf16.shape[1]
    return pl.pallas_call(
        kern,
        out_shape=(jax.ShapeDtypeStruct((m, dc), BF16), jax.ShapeDtypeStruct((CONV_HALO, dc), F32),
                   jax.ShapeDtypeStruct((m, da), BF16), jax.ShapeDtypeStruct((m, da), F32),
                   jax.ShapeDtypeStruct((m, da), F32), jax.ShapeDtypeStruct((m, da), BF16),
                   jax.ShapeDtypeStruct((xr.shape[0], n), F32)),
        grid=(n_tiles + 1,),
        in_specs=[row(d), const(shift), const(scale), const(xr), const(shift_r), const(scale_r), const(g_pre),
                  *w_specs, const(conv_w8), const(conv_b), const(g_cn), const(b_cn)],
        out_specs=(row(dc), pl.BlockSpec((CONV_HALO, dc), lambda i: (0, 0)),
                   row(da), row(da), row(da), row(da),
                   pl.BlockSpec((xr.shape[0], n), lambda i: (0, 0))),
        scratch_shapes=[pltpu.VMEM((bm + CONV_HALO, dc), F32),
                        pltpu.VMEM((SUBLANES - 1, conv_bt + CONV_HALO - SUBLANES, dc), F32),
                        pltpu.VMEM((bm, dc), F32)],
        compiler_params=_params("arbitrary"),
        name="in_proj",
    )(x, shift, scale, xr, shift_r, scale_r, g_pre, *([w_in_bf16] * len(segments)), conv_w8, conv_b, g_cn, b_cn)


def _conv_sample_kernel(state_ref, a_ref, b_ref, z_ref, w_ref, cb_ref, gcn_ref, bcn_ref,
                        o_ref, ns_ref, conv_ref):
    nb = state_ref.shape[0]
    w_hist = w_ref[0:CONV_STATE, :]
    u = a_ref[...] * _sigmoid(b_ref[...])
    for b in range(nb):
        st = state_ref[b]
        conv_ref[b:b + 1, :] = jnp.sum(st * w_hist, axis=0, keepdims=True)
        ns_ref[b, 0:CONV_STATE - 1, :] = state_ref[b, 1:CONV_STATE, :]
        ns_ref[b, CONV_STATE - 1:CONV_STATE, :] = u[b:b + 1, :]
    conv = conv_ref[...] + w_ref[CONV_STATE:CONV_WIDTH, :] * u + cb_ref[...]
    o_ref[...] = _ln_swish_gate(conv, gcn_ref, bcn_ref, _silu(z_ref[...])).astype(o_ref.dtype)


def _conv_sample(state, proj, conv_w, conv_b, g_cn, b_cn):
    nb, _, dc = state.shape
    const = lambda a: pl.BlockSpec(a.shape, lambda i: (0,) * a.ndim)
    col = lambda c: pl.BlockSpec((nb, dc), lambda i: (0, c))
    return pl.pallas_call(
        _conv_sample_kernel,
        out_shape=(jax.ShapeDtypeStruct((nb, dc), BF16),
                   jax.ShapeDtypeStruct((nb, CONV_STATE, dc), F32)),
        grid=(1,),
        in_specs=[const(state), col(0), col(1), col(2), const(conv_w), const(conv_b), const(g_cn), const(b_cn)],
        out_specs=(pl.BlockSpec((nb, dc), lambda i: (0, 0)),
                   pl.BlockSpec((nb, CONV_STATE, dc), lambda i: (0, 0, 0))),
        scratch_shapes=[pltpu.VMEM((nb, dc), F32)],
        compiler_params=_params("arbitrary"),
        name="conv_sample",
    )(state, proj, proj, proj, conv_w, conv_b, g_cn, b_cn)


def _lambda_full(lq1_ref, lk1_ref, lq2_ref, lk2_ref, layer_idx):
    s1 = jnp.sum(lq1_ref[...] * lk1_ref[...], axis=-1, keepdims=True)
    s2 = jnp.sum(lq2_ref[...] * lk2_ref[...], axis=-1, keepdims=True)
    return jnp.exp(s1) - jnp.exp(s2) + _lambda_init(layer_idx)


def _subln_gate(o, g_ref, gate, layer_idx):
    y = o * lax.rsqrt(jnp.mean(o * o, axis=-1, keepdims=True) + EPS) * g_ref[...]
    return y * (1.0 - _lambda_init(layer_idx)) * gate


def _trips_before(i):
    return (i * i) // 4


def _attn_kernel(pt_ref,
                 q_ref, k_ref, v_ref, ga_ref, g_ref, lq1_ref, lk1_ref, lq2_ref, lk2_ref,
                 qd_ref, knd_ref, vnd_ref, zd_ref, ck_hbm, cv_hbm,
                 o_ref, od_ref,
                 kb_ref, vt_ref, qs_ref, sa_ref, sb_ref, mxa_ref, mxb_ref, m_ref, l_ref, acc_ref,
                 kpg_ref, vpg_ref, ksem, vsem, md_ref, ld_ref, accd_ref,
                 *, bq, layer_idx, pages, trips_per_seq, n_trips):
    h = pl.program_id(0)
    i = pl.program_id(1)
    n_chunks = kb_ref.shape[0]
    hd = q_ref.shape[1]
    half = hd // 2
    nh = qd_ref.shape[1]
    n_slots = kpg_ref.shape[0]
    trips_per_head = _trips_before(n_chunks)

    def page_copies(trip, slot, from_table):
        copies = []
        for j in range(pages):
            page = pt_ref[trip * pages + j] if from_table else 0
            copies.append(pltpu.make_async_copy(ck_hbm.at[page], kpg_ref.at[slot, j], ksem.at[slot]))
            copies.append(pltpu.make_async_copy(cv_hbm.at[page], vpg_ref.at[slot, j], vsem.at[slot]))
        return copies

    @pl.when(jnp.logical_and(h == 0, i == 0))
    def _():
        for trip in range(n_slots - 1):
            for cp in page_copies(trip, trip, True):
                cp.start()

    @pl.when(i == 0)
    def _():
        for c in range(n_chunks):
            kb_ref[c] = k_ref[c * bq:(c + 1) * bq, :].astype(BF16)
            vt_ref[c] = v_ref[c * bq:(c + 1) * bq, :].T.astype(BF16)

    q_t = q_ref[...].astype(F32).T
    d_idx = lax.broadcasted_iota(jnp.int32, q_t.shape, 0)
    qs_ref[:, 0:bq] = jnp.where(d_idx < half, q_t, 0.0).astype(BF16)
    qs_ref[:, bq:2 * bq] = jnp.where(d_idx >= half, q_t, 0.0).astype(BF16)
    m_ref[...] = jnp.full(m_ref.shape, -jnp.inf, F32)
    l_ref[...] = jnp.zeros(l_ref.shape, F32)
    acc_ref[...] = jnp.zeros(acc_ref.shape, F32)

    tile_w = MXU_TILE
    n_tiles = 2 * bq // tile_w

    def scores_tile(c, s_ref, mx_ref, t, diagonal=False):
        cs = slice(t * tile_w, (t + 1) * tile_w)
        s = jnp.dot(kb_ref[c], qs_ref[:, cs], preferred_element_type=F32)
        if diagonal:
            kidx = lax.broadcasted_iota(jnp.int32, s.shape, 0)
            qidx = (lax.broadcasted_iota(jnp.int32, s.shape, 1) + t * tile_w) & (bq - 1)
            s = jnp.where(kidx <= qidx, s, -jnp.inf)
        s_ref[:, cs] = s
        mx_ref[:, cs] = jnp.max(s, axis=0, keepdims=True)

    def accumulate_tile(c, s_ref, mx_ref, t):
        cs = slice(t * tile_w, (t + 1) * tile_w)
        m_old = m_ref[:, cs]
        m_new = jnp.maximum(m_old, mx_ref[:, cs])
        p = jnp.exp2(s_ref[:, cs] - m_new)
        alpha = jnp.exp2(m_old - m_new)
        l_ref[:, cs] = alpha * l_ref[:, cs] + jnp.sum(p, axis=0, keepdims=True)
        pv = jnp.dot(vt_ref[c], p.astype(BF16), preferred_element_type=F32)
        acc_ref[:, cs] = alpha * acc_ref[:, cs] + pv
        m_ref[:, cs] = m_new

    def decode_query(b):
        q = qd_ref[b] * half ** -0.5
        lane = lax.broadcasted_iota(jnp.int32, q.shape, 1)
        qx = jnp.concatenate([jnp.where(lane < half, q, 0.0), jnp.where(lane >= half, q, 0.0)], axis=0)
        return qx.astype(BF16)

    def decode_scores(b, slot):
        rows = pages * kpg_ref.shape[2]
        col_head = lax.broadcasted_iota(jnp.int32, (2 * nh, rows), 1) & (nh - 1)
        row_head = lax.broadcasted_iota(jnp.int32, (2 * nh, rows), 0) & (nh - 1)
        k_all = kpg_ref[slot].reshape(rows, hd).astype(BF16)
        s = lax.dot_general(decode_query(b), k_all, NT_DIMS, preferred_element_type=F32)
        return jnp.where(col_head == row_head, s, -jnp.inf)

    def decode_update(s, slot, first):
        m_old = jnp.where(first, -jnp.inf, md_ref[...])
        l_old = jnp.where(first, 0.0, ld_ref[...])
        acc_old = jnp.where(first, 0.0, accd_ref[...])
        m_new = jnp.maximum(m_old, jnp.max(s, axis=-1, keepdims=True))
        alpha = jnp.exp(m_old - m_new)
        p = jnp.exp(s - m_new)
        v_all = vpg_ref[slot].reshape(s.shape[1], hd).astype(BF16)
        md_ref[...] = m_new
        ld_ref[...] = alpha * l_old + jnp.sum(p, axis=-1, keepdims=True)
        accd_ref[...] = alpha * acc_old + jnp.dot(p.astype(BF16), v_all, preferred_element_type=F32)

    def decode_finish(b, lam):
        qx = decode_query(b).astype(F32)
        kn = knd_ref[b].astype(BF16).astype(F32)
        vn = vnd_ref[b].astype(BF16).astype(F32)
        kn2 = jnp.concatenate([kn, kn], axis=0)
        vn2 = jnp.concatenate([vn, vn], axis=0)
        s_new = jnp.sum(qx * kn2, axis=-1, keepdims=True)
        m_old = md_ref[...]
        m_fin = jnp.maximum(m_old, s_new)
        a_fin = jnp.exp(m_old - m_fin)
        p_new = jnp.exp(s_new - m_fin)
        l_fin = a_fin * ld_ref[...] + p_new
        on = (a_fin * accd_ref[...] + p_new * vn2) / l_fin
        o = on[0:nh] - lam * on[nh:2 * nh]
        od_ref[b] = _subln_gate(o, g_ref, _silu(zd_ref[b]), layer_idx)

    for t in range(n_tiles):
        scores_tile(i, sa_ref, mxa_ref, t, diagonal=True)
    trip0 = h * trips_per_head + _trips_before(i)

    def body(p, carry):
        trip = trip0 + p
        slot = lax.rem(trip, n_slots)
        for cp in page_copies(trip, slot, False):
            cp.wait()

        @pl.when(trip + n_slots - 1 < n_trips)
        def _():
            nxt = trip + n_slots - 1
            for cp in page_copies(nxt, lax.rem(nxt, n_slots), True):
                cp.start()

        seq = trip // trips_per_seq
        part = lax.rem(trip, trips_per_seq)
        c_cur = jnp.where(p == 0, i, 2 * p - 1)
        for t in range(n_tiles):
            scores_tile(2 * p, sb_ref, mxb_ref, t)
            accumulate_tile(c_cur, sa_ref, mxa_ref, t)
        s_dec = decode_scores(seq, slot)
        c_next = jnp.minimum(2 * p + 1, i - 1)
        for t in range(n_tiles):
            scores_tile(c_next, sa_ref, mxa_ref, t)
            accumulate_tile(2 * p, sb_ref, mxb_ref, t)
        decode_update(s_dec, slot, part == 0)

        @pl.when(part == trips_per_seq - 1)
        def _():
            decode_finish(seq, _lambda_full(lq1_ref, lk1_ref, lq2_ref, lk2_ref, layer_idx))

        return carry

    lax.fori_loop(0, (i + 1) // 2, body, 0)

    @pl.when(i % 2 == 0)
    def _():
        for t in range(n_tiles):
            accumulate_tile(jnp.maximum(i - 1, 0), sa_ref, mxa_ref, t)

    lam = _lambda_full(lq1_ref, lk1_ref, lq2_ref, lk2_ref, layer_idx)
    acc = acc_ref[...]
    inv_l = 1.0 / l_ref[...]
    o_t = acc[:, 0:bq] * inv_l[:, 0:bq] - lam * (acc[:, bq:2 * bq] * inv_l[:, bq:2 * bq])
    o = o_t.T
    o_ref[...] = _subln_gate(o, g_ref, ga_ref[...].astype(F32), layer_idx).astype(o_ref.dtype)


def _attention(q, k, v, ga, g_subln, lq1, lk1, lq2, lk2,
               page_table, proj_d, att_col, cache_k, cache_v, layer_idx, bq=512, n_slots=3):
    s, da = q.shape
    hd = da // N_HEADS
    nq = s // bq
    nb, n_pages = page_table.shape
    n_pool, page, nh, _ = cache_k.shape
    ck = cache_k.reshape(n_pool, page * nh, hd)
    cv = cache_v.reshape(n_pool, page * nh, hd)
    pt = page_table.reshape(nb * n_pages)
    assert att_col % da == 0
    proj_heads = proj_d.reshape(nb, proj_d.shape[1] // hd, hd)
    first_blk = att_col // da

    n_trips = N_HEADS * _trips_before(nq)
    pages = (nb * n_pages) // n_trips
    assert pages * n_trips == nb * n_pages and n_pages % pages == 0, (n_trips, nb, n_pages)
    assert n_trips >= n_slots
    trips_per_seq = n_pages // pages

    kern = functools.partial(_attn_kernel, bq=bq, layer_idx=layer_idx, pages=pages,
                             trips_per_seq=trips_per_seq, n_trips=n_trips)
    tile = pl.BlockSpec((bq, hd), lambda h, i, pt: (i, h))
    full = pl.BlockSpec((s, hd), lambda h, i, pt: (0, h))
    small = lambda a: pl.BlockSpec(a.shape, lambda h, i, pt: (0, 0))
    whole = pl.BlockSpec((nb, nh, hd), lambda h, i, pt: (0, 0, 0))
    dec = lambda blk: pl.BlockSpec((nb, nh, hd), lambda h, i, pt: (0, first_blk + blk, 0))
    hbm = pl.BlockSpec(memory_space=pl.ANY)
    grid_spec = pltpu.PrefetchScalarGridSpec(
        num_scalar_prefetch=1,
        grid=(N_HEADS, nq),
        in_specs=[tile, full, full, tile, small(g_subln), small(lq1), small(lk1), small(lq2), small(lk2),
                  dec(0), dec(1), dec(2), dec(3), hbm, hbm],
        out_specs=(tile, whole),
        scratch_shapes=[pltpu.VMEM((nq, bq, hd), BF16),
                        pltpu.VMEM((nq, hd, bq), BF16),
                        pltpu.VMEM((hd, 2 * bq), BF16),
                        pltpu.VMEM((bq, 2 * bq), F32), pltpu.VMEM((bq, 2 * bq), F32),
                        pltpu.VMEM((1, 2 * bq), F32), pltpu.VMEM((1, 2 * bq), F32),
                        pltpu.VMEM((1, 2 * bq), F32), pltpu.VMEM((1, 2 * bq), F32),
                        pltpu.VMEM((hd, 2 * bq), F32),
                        pltpu.VMEM((n_slots, pages, page * nh, hd), F32),
                        pltpu.VMEM((n_slots, pages, page * nh, hd), F32),
                        pltpu.SemaphoreType.DMA((n_slots,)), pltpu.SemaphoreType.DMA((n_slots,)),
                        pltpu.VMEM((2 * nh, 1), F32), pltpu.VMEM((2 * nh, 1), F32),
                        pltpu.VMEM((2 * nh, hd), F32)],
    )
    return pl.pallas_call(
        kern,
        out_shape=(jax.ShapeDtypeStruct((s, da), BF16), jax.ShapeDtypeStruct((nb, nh, hd), F32)),
        grid_spec=grid_spec,
        compiler_params=_params("arbitrary", "arbitrary"),
        name="attention",
    )(pt, q, k, v, ga, g_subln, lq1, lk1, lq2, lk2,
      proj_heads, proj_heads, proj_heads, proj_heads, ck, cv)


def _out_kernel(oc_ref, oa_ref, wc_ref, wa_ref, x_ref, gate_ref, g_ref, o_ref):
    y = jnp.dot(oc_ref[...].astype(BF16), wc_ref[...], preferred_element_type=F32)
    y = y + jnp.dot(oa_ref[...].astype(BF16), wa_ref[...], preferred_element_type=F32)
    yn = y * lax.rsqrt(jnp.mean(y * y, axis=-1, keepdims=True) + EPS) * g_ref[...]
    o_ref[...] = x_ref[...] + gate_ref[...] * yn


def _out_proj(oc, oa, w_out_bf16, x, gate, g_post, bm):
    m, d = x.shape
    dc = oc.shape[1]
    da = oa.shape[1]
    per_row = gate.shape[0] != 1
    assert dc % da == 0
    gate_spec = pl.BlockSpec((bm, d), lambda i: (i, 0)) if per_row else pl.BlockSpec((1, d), lambda i: (0, 0))
    return pl.pallas_call(
        _out_kernel,
        out_shape=jax.ShapeDtypeStruct((m, d), F32),
        grid=(m // bm,),
        in_specs=[pl.BlockSpec((bm, dc), lambda i: (i, 0)),
                  pl.BlockSpec((bm, da), lambda i: (i, 0)),
                  pl.BlockSpec((dc, d), lambda i: (0, 0)),
                  pl.BlockSpec((da, d), lambda i: (dc // da, 0)),
                  pl.BlockSpec((bm, d), lambda i: (i, 0)),
                  gate_spec,
                  pl.BlockSpec((1, d), lambda i: (0, 0))],
        out_specs=pl.BlockSpec((bm, d), lambda i: (i, 0)),
        compiler_params=_params("arbitrary"),
        name="out_proj",
    )(oc, oa, w_out_bf16, w_out_bf16, x, gate, g_post)


def kernel(x_prompt, x_sample, cache_k, cache_v, state_conv, page_table, c_prompt, c_sample,
           w_ada, b_ada, g_pre, g_post, w_in, conv_w, conv_b, g_cn, b_cn,
           lq1, lk1, lq2, lk2, g_subln, w_out):
    depth = w_ada.shape[0]
    bp, seq, d = x_prompt.shape
    nb = x_sample.shape[0]
    assert bp == 1 and x_sample.shape[1] == 1
    dc = conv_w.shape[2]
    da = w_out.shape[1] - dc
    hd = da // N_HEADS
    dk = hd // 2

    xp = x_prompt.reshape(seq, d)
    xs = x_sample.reshape(nb, d)
    pad = (-(bp + nb)) % 8
    c_all = jnp.concatenate([c_prompt, c_sample, jnp.zeros((pad, d), F32)], axis=0)
    row = lambda a: a.reshape(1, -1)

    kp_l, vp_l, cp_l, ks_l, vs_l, cs_l = [], [], [], [], [], []
    for l in range(depth):
        w_in_b = w_in[l].astype(BF16)
        w_out_b = w_out[l].astype(BF16)
        mod = _modulation(c_all, w_ada[l], row(b_ada[l]))
        shift, scale, gate = mod[:, 0:d], mod[:, d:2 * d], mod[:, 2 * d:3 * d]
        vecs = (row(conv_b[l]), row(g_cn[l]), row(b_cn[l]))
        lams = (row(lq1[l]), row(lk1[l]), row(lq2[l]), row(lk2[l]))
        g_sub = row(g_subln[l])
        conv_w8 = jnp.broadcast_to(conv_w[l][:, None, :], (CONV_WIDTH, SUBLANES, dc))

        sl = slice(bp, bp + nb)
        oc, u_tail, q, k, v, ga, proj_s = _in_proj(
            xp, shift[0:1], scale[0:1], xs, shift[sl], scale[sl], row(g_pre[l]), w_in_b, conv_w8, *vecs,
            dc=dc, da=da, q_scale=dk ** -0.5 * math.log2(math.e))
        att_col = 3 * dc
        ksn = proj_s[:, att_col + da:att_col + 2 * da]
        vsn = proj_s[:, att_col + 2 * da:att_col + 3 * da]

        ocs, new_state = _conv_sample(state_conv[l], proj_s, conv_w[l], *vecs)

        oa, oas = _attention(q, k, v, ga, g_sub, *lams, page_table, proj_s, att_col,
                             cache_k[l], cache_v[l], layer_idx=l)

        xp = _out_proj(oc, oa, w_out_b, xp, gate[0:1], row(g_post[l]), bm=512)
        xs = _out_proj(ocs, oas.reshape(nb, da), w_out_b, xs, gate[sl], row(g_post[l]), bm=nb)
        kp_l.append(k.reshape(bp, seq, N_HEADS, hd))
        vp_l.append(v.reshape(bp, seq, N_HEADS, hd))
        cp_l.append(u_tail[CONV_HALO - CONV_STATE:].reshape(bp, CONV_STATE, dc))
        ks_l.append(ksn.reshape(nb, 1, N_HEADS, hd))
        vs_l.append(vsn.reshape(nb, 1, N_HEADS, hd))
        cs_l.append(new_state)

    return (xp.reshape(bp, seq, d), xs.reshape(nb, 1, d),
            jnp.stack(kp_l), jnp.stack(vp_l), jnp.stack(cp_l),
            jnp.stack(ks_l), jnp.stack(vs_l), jnp.stack(cs_l))
```

```python
import functools
import math

import jax
import jax.numpy as jnp
from jax import lax
from jax.experimental import pallas as pl
from jax.experimental.pallas import tpu as pltpu

F32 = jnp.float32
BF16 = jnp.bfloat16

N_HEADS = 8
CONV_WIDTH = 31
CONV_STATE = CONV_WIDTH - 1
CONV_HALO = 32
EPS = 1e-6

LANES = 128
SUBLANES = 8
MXU_TILE = 256
VMEM_LIMIT = 56 * 1024 * 1024

NT_DIMS = (((1,), (1,)), ((), ()))


def _lambda_init(layer_idx):
    return 0.8 - 0.6 * math.exp(-0.3 * layer_idx)


def _params(*sem):
    return pltpu.CompilerParams(dimension_semantics=sem, vmem_limit_bytes=VMEM_LIMIT)


def _sigmoid(x):
    return 1.0 / (1.0 + jnp.exp(-x))


def _silu(x):
    return x * _sigmoid(x)


def _mod_kernel(c_ref, w_ref, b_ref, o_ref):
    a = _silu(c_ref[...]).astype(BF16)
    o_ref[...] = jnp.dot(a, w_ref[...].astype(BF16), preferred_element_type=F32) + b_ref[...]


def _modulation(c_all, w_ada, b_ada, bn=512):
    rows, d = c_all.shape
    n = w_ada.shape[1]
    return pl.pallas_call(
        _mod_kernel,
        out_shape=jax.ShapeDtypeStruct((rows, n), F32),
        grid=(n // bn,),
        in_specs=[pl.BlockSpec((rows, d), lambda j: (0, 0)),
                  pl.BlockSpec((d, bn), lambda j: (0, j)),
                  pl.BlockSpec((1, bn), lambda j: (0, j))],
        out_specs=pl.BlockSpec((rows, bn), lambda j: (0, j)),
        compiler_params=_params("arbitrary"),
        name="adaln_mod",
    )(c_all, w_ada, b_ada)


def _ln_swish_gate(conv, g_ref, b_ref, gate):
    mu = jnp.mean(conv, axis=-1, keepdims=True)
    d = conv - mu
    var = jnp.mean(d * d, axis=-1, keepdims=True)
    y = d * lax.rsqrt(var + EPS) * g_ref[...] + b_ref[...]
    return _silu(y) * gate


def _causal_conv_rows(ext_ref, sh_ref, conv_ref, w_ref, cb_ref, row0, bt, rows=32):
    span = sh_ref.shape[1]
    for b in range(1, SUBLANES):
        sh_ref[b - 1] = ext_ref[row0 + b:row0 + b + span, :]
    off = CONV_HALO - CONV_STATE
    reps = rows // SUBLANES
    for c in range(ext_ref.shape[1] // LANES):
        cs = slice(c * LANES, (c + 1) * LANES)
        taps = [jnp.concatenate([w_ref[j, :, cs]] * reps, axis=0) for j in range(CONV_WIDTH)]
        bias = jnp.broadcast_to(cb_ref[:, cs], (rows, LANES))
        for r in range(bt // rows):
            acc = bias
            for j in range(CONV_WIDTH):
                a, b = divmod(off + j, SUBLANES)
                base = r * rows + SUBLANES * a
                if b == 0:
                    src = ext_ref[row0 + base:row0 + base + rows, cs]
                else:
                    src = sh_ref[b - 1, base:base + rows, cs]
                acc = acc + taps[j] * src
            conv_ref[row0 + r * rows:row0 + (r + 1) * rows, cs] = acc


def _modulated_norm(x_ref, shift_ref, scale_ref, g_ref):
    x = x_ref[...]
    y = x * lax.rsqrt(jnp.mean(x * x, axis=-1, keepdims=True) + EPS) * g_ref[...]
    return (y * (1.0 + scale_ref[...]) + shift_ref[...]).astype(BF16)


def _proj_kernel(x_ref, shift_ref, scale_ref, xr_ref, shiftr_ref, scaler_ref, g_ref,
                 wa_ref, wb_ref, wzc_ref, wq_ref, wk_ref, wv_ref, wza_ref,
                 cw_ref, cb_ref, gcn_ref, bcn_ref,
                 oc_ref, utail_ref, q_ref, k_ref, v_ref, ga_ref, projr_ref, ext_ref, sh_ref, conv_ref,
                 *, q_scale):
    step = pl.program_id(0)
    last = pl.num_programs(0) - 1
    w_refs = (wa_ref, wb_ref, wzc_ref, wq_ref, wk_ref, wv_ref, wza_ref)

    @pl.when(step < last)
    def _():
        h = _modulated_norm(x_ref, shift_ref, scale_ref, g_ref)
        bm = h.shape[0]
        dc = oc_ref.shape[1]

        def seg(w_ref):
            return jnp.dot(h, w_ref[...], preferred_element_type=F32)

        u = seg(wa_ref) * _sigmoid(seg(wb_ref))
        gate_c = _silu(seg(wzc_ref))

        @pl.when(step == 0)
        def _():
            ext_ref[0:CONV_HALO, :] = jnp.zeros((CONV_HALO, dc), F32)

        ext_ref[CONV_HALO:CONV_HALO + bm, :] = u
        utail_ref[...] = u[bm - CONV_HALO:bm, :]
        bt = sh_ref.shape[1] + SUBLANES - CONV_HALO
        for row0 in range(0, bm, bt):
            _causal_conv_rows(ext_ref, sh_ref, conv_ref, cw_ref, cb_ref, row0, bt)
        oc_ref[...] = _ln_swish_gate(conv_ref[...], gcn_ref, bcn_ref, gate_c).astype(oc_ref.dtype)
        ext_ref[0:CONV_HALO, :] = ext_ref[bm:bm + CONV_HALO, :]

        q_ref[...] = (seg(wq_ref) * q_scale).astype(q_ref.dtype)
        k_ref[...] = seg(wk_ref)
        v_ref[...] = seg(wv_ref)
        ga_ref[...] = _silu(seg(wza_ref)).astype(ga_ref.dtype)

    @pl.when(step == last)
    def _():
        hr = _modulated_norm(xr_ref, shiftr_ref, scaler_ref, g_ref)
        start = 0
        for w_ref in w_refs:
            width = w_ref.shape[1]
            projr_ref[:, start:start + width] = jnp.dot(hr, w_ref[...], preferred_element_type=F32)
            start += width


def _in_proj(x, shift, scale, xr, shift_r, scale_r, g_pre, w_in_bf16, conv_w8, conv_b, g_cn, b_cn,
             dc, da, q_scale, bm=256, conv_bt=128):
    m, d = x.shape
    assert bm % conv_bt == 0 and shift.shape[0] == 1
    n_tiles = m // bm
    kern = functools.partial(_proj_kernel, q_scale=q_scale)
    row = lambda w: pl.BlockSpec((bm, w), lambda i: (jnp.minimum(i, n_tiles - 1), 0))
    const = lambda a: pl.BlockSpec(a.shape, lambda i: (0,) * a.ndim)
    segments = [(t * dc, dc) for t in range(3)] + [(3 * dc + t * da, da) for t in range(4)]
    assert all(start % width == 0 for start, width in segments) and segments[-1][0] + da == w_in_bf16.shape[1]
    w_specs = [pl.BlockSpec((d, width), lambda i, blk=start // width: (0, blk), pipeline_mode=pl.Buffered(1))
               for start, width in segments]
    n = w_in_bf16.shape[1]
    return pl.pallas_call(
        kern,
        out_shape=(jax.ShapeDtypeStruct((m, dc), BF16), jax.ShapeDtypeStruct((CONV_HALO, dc), F32),
                   jax.ShapeDtypeStruct((m, da), BF16), jax.ShapeDtypeStruct((m, da), F32),
                   jax.ShapeDtypeStruct((m, da), F32), jax.ShapeDtypeStruct((m, da), BF16),
                   jax.ShapeDtypeStruct((xr.shape[0], n), F32)),
        grid=(n_tiles + 1,),
        in_specs=[row(d), const(shift), const(scale), const(xr), const(shift_r), const(scale_r), const(g_pre),
                  *w_specs, const(conv_w8), const(conv_b), const(g_cn), const(b_cn)],
        out_specs=(row(dc), pl.BlockSpec((CONV_HALO, dc), lambda i: (0, 0)),
                   row(da), row(da), row(da), row(da),
                   pl.BlockSpec((xr.shape[0], n), lambda i: (0, 0))),
        scratch_shapes=[pltpu.VMEM((bm + CONV_HALO, dc), F32),
                        pltpu.VMEM((SUBLANES - 1, conv_bt + CONV_HALO - SUBLANES, dc), F32),
                        pltpu.VMEM((bm, dc), F32)],
        compiler_params=_params("arbitrary"),
        name="in_proj",
    )(x, shift, scale, xr, shift_r, scale_r, g_pre, *([w_in_bf16] * len(segments)), conv_w8, conv_b, g_cn, b_cn)


def _conv_sample_kernel(state_ref, a_ref, b_ref, z_ref, w_ref, cb_ref, gcn_ref, bcn_ref,
                        o_ref, ns_ref, conv_ref):
    nb = state_ref.shape[0]
    w_hist = w_ref[0:CONV_STATE, :]
    u = a_ref[...] * _sigmoid(b_ref[...])
    for b in range(nb):
        st = state_ref[b]
        conv_ref[b:b + 1, :] = jnp.sum(st * w_hist, axis=0, keepdims=True)
        ns_ref[b, 0:CONV_STATE - 1, :] = state_ref[b, 1:CONV_STATE, :]
        ns_ref[b, CONV_STATE - 1:CONV_STATE, :] = u[b:b + 1, :]
    conv = conv_ref[...] + w_ref[CONV_STATE:CONV_WIDTH, :] * u + cb_ref[...]
    o_ref[...] = _ln_swish_gate(conv, gcn_ref, bcn_ref, _silu(z_ref[...])).astype(o_ref.dtype)


def _conv_sample(state, proj, conv_w, conv_b, g_cn, b_cn):
    nb, _, dc = state.shape
    const = lambda a: pl.BlockSpec(a.shape, lambda i: (0,) * a.ndim)
    col = lambda c: pl.BlockSpec((nb, dc), lambda i: (0, c))
    return pl.pallas_call(
        _conv_sample_kernel,
        out_shape=(jax.ShapeDtypeStruct((nb, dc), BF16),
                   jax.ShapeDtypeStruct((nb, CONV_STATE, dc), F32)),
        grid=(1,),
        in_specs=[const(state), col(0), col(1), col(2), const(conv_w), const(conv_b), const(g_cn), const(b_cn)],
        out_specs=(pl.BlockSpec((nb, dc), lambda i: (0, 0)),
                   pl.BlockSpec((nb, CONV_STATE, dc), lambda i: (0, 0, 0))),
        scratch_shapes=[pltpu.VMEM((nb, dc), F32)],
        compiler_params=_params("arbitrary"),
        name="conv_sample",
    )(state, proj, proj, proj, conv_w, conv_b, g_cn, b_cn)


def _lambda_full(lq1_ref, lk1_ref, lq2_ref, lk2_ref, layer_idx):
    s1 = jnp.sum(lq1_ref[...] * lk1_ref[...], axis=-1, keepdims=True)
    s2 = jnp.sum(lq2_ref[...] * lk2_ref[...], axis=-1, keepdims=True)
    return jnp.exp(s1) - jnp.exp(s2) + _lambda_init(layer_idx)


def _subln_gate(o, g_ref, gate, layer_idx):
    y = o * lax.rsqrt(jnp.mean(o * o, axis=-1, keepdims=True) + EPS) * g_ref[...]
    return y * (1.0 - _lambda_init(layer_idx)) * gate


def _trips_before(i):
    return (i * i) // 4


def _attn_kernel(pt_ref,
                 q_ref, k_ref, v_ref, ga_ref, g_ref, lq1_ref, lk1_ref, lq2_ref, lk2_ref,
                 qd_ref, knd_ref, vnd_ref, zd_ref, ck_hbm, cv_hbm,
                 o_ref, od_ref,
                 kb_ref, vt_ref, qs_ref, sa_ref, sb_ref, mxa_ref, mxb_ref, m_ref, l_ref, acc_ref,
                 kpg_ref, vpg_ref, ksem, vsem, md_ref, ld_ref, accd_ref,
                 *, bq, layer_idx, pages, trips_per_seq, n_trips):
    h = pl.program_id(0)
    i = pl.program_id(1)
    n_chunks = kb_ref.shape[0]
    hd = q_ref.shape[1]
    half = hd // 2
    nh = qd_ref.shape[1]
    n_slots = kpg_ref.shape[0]
    trips_per_head = _trips_before(n_chunks)

    def page_copies(trip, slot, from_table):
        copies = []
        for j in range(pages):
            page = pt_ref[trip * pages + j] if from_table else 0
            copies.append(pltpu.make_async_copy(ck_hbm.at[page], kpg_ref.at[slot, j], ksem.at[slot]))
            copies.append(pltpu.make_async_copy(cv_hbm.at[page], vpg_ref.at[slot, j], vsem.at[slot]))
        return copies

    @pl.when(jnp.logical_and(h == 0, i == 0))
    def _():
        for trip in range(n_slots - 1):
            for cp in page_copies(trip, trip, True):
                cp.start()

    @pl.when(i == 0)
    def _():
        for c in range(n_chunks):
            kb_ref[c] = k_ref[c * bq:(c + 1) * bq, :].astype(BF16)
            vt_ref[c] = v_ref[c * bq:(c + 1) * bq, :].T.astype(BF16)

    q_t = q_ref[...].astype(F32).T
    d_idx = lax.broadcasted_iota(jnp.int32, q_t.shape, 0)
    qs_ref[:, 0:bq] = jnp.where(d_idx < half, q_t, 0.0).astype(BF16)
    qs_ref[:, bq:2 * bq] = jnp.where(d_idx >= half, q_t, 0.0).astype(BF16)
    m_ref[...] = jnp.full(m_ref.shape, -jnp.inf, F32)
    l_ref[...] = jnp.zeros(l_ref.shape, F32)
    acc_ref[...] = jnp.zeros(acc_ref.shape, F32)

    tile_w = MXU_TILE
    n_tiles = 2 * bq // tile_w

    def scores_tile(c, s_ref, mx_ref, t, diagonal=False):
        cs = slice(t * tile_w, (t + 1) * tile_w)
        s = jnp.dot(kb_ref[c], qs_ref[:, cs], preferred_element_type=F32)
        if diagonal:
            kidx = lax.broadcasted_iota(jnp.int32, s.shape, 0)
            qidx = (lax.broadcasted_iota(jnp.int32, s.shape, 1) + t * tile_w) & (bq - 1)
            s = jnp.where(kidx <= qidx, s, -jnp.inf)
        s_ref[:, cs] = s
        mx_ref[:, cs] = jnp.max(s, axis=0, keepdims=True)

    def accumulate_tile(c, s_ref, mx_ref, t):
        cs = slice(t * tile_w, (t + 1) * tile_w)
        m_old = m_ref[:, cs]
        m_new = jnp.maximum(m_old, mx_ref[:, cs])
        p = jnp.exp2(s_ref[:, cs] - m_new)
        alpha = jnp.exp2(m_old - m_new)
        l_ref[:, cs] = alpha * l_ref[:, cs] + jnp.sum(p, axis=0, keepdims=True)
        pv = jnp.dot(vt_ref[c], p.astype(BF16), preferred_element_type=F32)
        acc_ref[:, cs] = alpha * acc_ref[:, cs] + pv
        m_ref[:, cs] = m_new

    def decode_query(b):
        q = qd_ref[b] * half ** -0.5
        lane = lax.broadcasted_iota(jnp.int32, q.shape, 1)
        qx = jnp.concatenate([jnp.where(lane < half, q, 0.0), jnp.where(lane >= half, q, 0.0)], axis=0)
        return qx.astype(BF16)

    def decode_scores(b, slot):
        rows = pages * kpg_ref.shape[2]
        col_head = lax.broadcasted_iota(jnp.int32, (2 * nh, rows), 1) & (nh - 1)
        row_head = lax.broadcasted_iota(jnp.int32, (2 * nh, rows), 0) & (nh - 1)
        k_all = kpg_ref[slot].reshape(rows, hd).astype(BF16)
        s = lax.dot_general(decode_query(b), k_all, NT_DIMS, preferred_element_type=F32)
        return jnp.where(col_head == row_head, s, -jnp.inf)

    def decode_update(s, slot, first):
        m_old = jnp.where(first, -jnp.inf, md_ref[...])
        l_old = jnp.where(first, 0.0, ld_ref[...])
        acc_old = jnp.where(first, 0.0, accd_ref[...])
        m_new = jnp.maximum(m_old, jnp.max(s, axis=-1, keepdims=True))
        alpha = jnp.exp(m_old - m_new)
        p = jnp.exp(s - m_new)
        v_all = vpg_ref[slot].reshape(s.shape[1], hd).astype(BF16)
        md_ref[...] = m_new
        ld_ref[...] = alpha * l_old + jnp.sum(p, axis=-1, keepdims=True)
        accd_ref[...] = alpha * acc_old + jnp.dot(p.astype(BF16), v_all, preferred_element_type=F32)

    def decode_finish(b, lam):
        qx = decode_query(b).astype(F32)
        kn = knd_ref[b].astype(BF16).astype(F32)
        vn = vnd_ref[b].astype(BF16).astype(F32)
        kn2 = jnp.concatenate([kn, kn], axis=0)
        vn2 = jnp.concatenate([vn, vn], axis=0)
        s_new = jnp.sum(qx * kn2, axis=-1, keepdims=True)
        m_old = md_ref[...]
        m_fin = jnp.maximum(m_old, s_new)
        a_fin = jnp.exp(m_old - m_fin)
        p_new = jnp.exp(s_new - m_fin)
        l_fin = a_fin * ld_ref[...] + p_new
        on = (a_fin * accd_ref[...] + p_new * vn2) / l_fin
        o = on[0:nh] - lam * on[nh:2 * nh]
        od_ref[b] = _subln_gate(o, g_ref, _silu(zd_ref[b]), layer_idx)

    for t in range(n_tiles):
        scores_tile(i, sa_ref, mxa_ref, t, diagonal=True)
    trip0 = h * trips_per_head + _trips_before(i)

    def body(p, carry):
        trip = trip0 + p
        slot = lax.rem(trip, n_slots)
        for cp in page_copies(trip, slot, False):
            cp.wait()

        @pl.when(trip + n_slots - 1 < n_trips)
        def _():
            nxt = trip + n_slots - 1
            for cp in page_copies(nxt, lax.rem(nxt, n_slots), True):
                cp.start()

        seq = trip // trips_per_seq
        part = lax.rem(trip, trips_per_seq)
        c_cur = jnp.where(p == 0, i, 2 * p - 1)
        for t in range(n_tiles):
            scores_tile(2 * p, sb_ref, mxb_ref, t)
            accumulate_tile(c_cur, sa_ref, mxa_ref, t)
        s_dec = decode_scores(seq, slot)
        c_next = jnp.minimum(2 * p + 1, i - 1)
        for t in range(n_tiles):
            scores_tile(c_next, sa_ref, mxa_ref, t)
            accumulate_tile(2 * p, sb_ref, mxb_ref, t)
        decode_update(s_dec, slot, part == 0)

        @pl.when(part == trips_per_seq - 1)
        def _():
            decode_finish(seq, _lambda_full(lq1_ref, lk1_ref, lq2_ref, lk2_ref, layer_idx))

        return carry

    lax.fori_loop(0, (i + 1) // 2, body, 0)

    @pl.when(i % 2 == 0)
    def _():
        for t in range(n_tiles):
            accumulate_tile(jnp.maximum(i - 1, 0), sa_ref, mxa_ref, t)

    lam = _lambda_full(lq1_ref, lk1_ref, lq2_ref, lk2_ref, layer_idx)
    acc = acc_ref[...]
    inv_l = 1.0 / l_ref[...]
    o_t = acc[:, 0:bq] * inv_l[:, 0:bq] - lam * (acc[:, bq:2 * bq] * inv_l[:, bq:2 * bq])
    o = o_t.T
    o_ref[...] = _subln_gate(o, g_ref, ga_ref[...].astype(F32), layer_idx).astype(o_ref.dtype)


def _attention(q, k, v, ga, g_subln, lq1, lk1, lq2, lk2,
               page_table, proj_d, att_col, cache_k, cache_v, layer_idx, bq=512, n_slots=4):
    s, da = q.shape
    hd = da // N_HEADS
    nq = s // bq
    nb, n_pages = page_table.shape
    n_pool, page, nh, _ = cache_k.shape
    ck = cache_k.reshape(n_pool, page * nh, hd)
    cv = cache_v.reshape(n_pool, page * nh, hd)
    pt = page_table.reshape(nb * n_pages)
    assert att_col % da == 0
    proj_heads = proj_d.reshape(nb, proj_d.shape[1] // hd, hd)
    first_blk = att_col // da

    n_trips = N_HEADS * _trips_before(nq)
    pages = (nb * n_pages) // n_trips
    assert pages * n_trips == nb * n_pages and n_pages % pages == 0, (n_trips, nb, n_pages)
    assert n_trips >= n_slots
    trips_per_seq = n_pages // pages

    kern = functools.partial(_attn_kernel, bq=bq, layer_idx=layer_idx, pages=pages,
                             trips_per_seq=trips_per_seq, n_trips=n_trips)
    tile = pl.BlockSpec((bq, hd), lambda h, i, pt: (i, h))
    full = pl.BlockSpec((s, hd), lambda h, i, pt: (0, h))
    small = lambda a: pl.BlockSpec(a.shape, lambda h, i, pt: (0, 0))
    whole = pl.BlockSpec((nb, nh, hd), lambda h, i, pt: (0, 0, 0))
    dec = lambda blk: pl.BlockSpec((nb, nh, hd), lambda h, i, pt: (0, first_blk + blk, 0))
    hbm = pl.BlockSpec(memory_space=pl.ANY)
    grid_spec = pltpu.PrefetchScalarGridSpec(
        num_scalar_prefetch=1,
        grid=(N_HEADS, nq),
        in_specs=[tile, full, full, tile, small(g_subln), small(lq1), small(lk1), small(lq2), small(lk2),
                  dec(0), dec(1), dec(2), dec(3), hbm, hbm],
        out_specs=(tile, whole),
        scratch_shapes=[pltpu.VMEM((nq, bq, hd), BF16),
                        pltpu.VMEM((nq, hd, bq), BF16),
                        pltpu.VMEM((hd, 2 * bq), BF16),
                        pltpu.VMEM((bq, 2 * bq), F32), pltpu.VMEM((bq, 2 * bq), F32),
                        pltpu.VMEM((1, 2 * bq), F32), pltpu.VMEM((1, 2 * bq), F32),
                        pltpu.VMEM((1, 2 * bq), F32), pltpu.VMEM((1, 2 * bq), F32),
                        pltpu.VMEM((hd, 2 * bq), F32),
                        pltpu.VMEM((n_slots, pages, page * nh, hd), F32),
                        pltpu.VMEM((n_slots, pages, page * nh, hd), F32),
                        pltpu.SemaphoreType.DMA((n_slots,)), pltpu.SemaphoreType.DMA((n_slots,)),
                        pltpu.VMEM((2 * nh, 1), F32), pltpu.VMEM((2 * nh, 1), F32),
                        pltpu.VMEM((2 * nh, hd), F32)],
    )
    return pl.pallas_call(
        kern,
        out_shape=(jax.ShapeDtypeStruct((s, da), BF16), jax.ShapeDtypeStruct((nb, nh, hd), F32)),
        grid_spec=grid_spec,
        compiler_params=_params("arbitrary", "arbitrary"),
        name="attention",
    )(pt, q, k, v, ga, g_subln, lq1, lk1, lq2, lk2,
      proj_heads, proj_heads, proj_heads, proj_heads, ck, cv)


def _out_kernel(oc_ref, oa_ref, wc_ref, wa_ref, x_ref, gate_ref, g_ref, o_ref):
    y = jnp.dot(oc_ref[...].astype(BF16), wc_ref[...], preferred_element_type=F32)
    y = y + jnp.dot(oa_ref[...].astype(BF16), wa_ref[...], preferred_element_type=F32)
    yn = y * lax.rsqrt(jnp.mean(y * y, axis=-1, keepdims=True) + EPS) * g_ref[...]
    o_ref[...] = x_ref[...] + gate_ref[...] * yn


def _out_proj(oc, oa, w_out_bf16, x, gate, g_post, bm):
    m, d = x.shape
    dc = oc.shape[1]
    da = oa.shape[1]
    per_row = gate.shape[0] != 1
    assert dc % da == 0
    gate_spec = pl.BlockSpec((bm, d), lambda i: (i, 0)) if per_row else pl.BlockSpec((1, d), lambda i: (0, 0))
    return pl.pallas_call(
        _out_kernel,
        out_shape=jax.ShapeDtypeStruct((m, d), F32),
        grid=(m // bm,),
        in_specs=[pl.BlockSpec((bm, dc), lambda i: (i, 0)),
                  pl.BlockSpec((bm, da), lambda i: (i, 0)),
                  pl.BlockSpec((dc, d), lambda i: (0, 0)),
                  pl.BlockSpec((da, d), lambda i: (dc // da, 0)),
                  pl.BlockSpec((bm, d), lambda i: (i, 0)),
                  gate_spec,
                  pl.BlockSpec((1, d), lambda i: (0, 0))],
        out_specs=pl.BlockSpec((bm, d), lambda i: (i, 0)),
        compiler_params=_params("arbitrary"),
        name="out_proj",
    )(oc, oa, w_out_bf16, w_out_bf16, x, gate, g_post)


def kernel(x_prompt, x_sample, cache_k, cache_v, state_conv, page_table, c_prompt, c_sample,
           w_ada, b_ada, g_pre, g_post, w_in, conv_w, conv_b, g_cn, b_cn,
           lq1, lk1, lq2, lk2, g_subln, w_out):
    depth = w_ada.shape[0]
    bp, seq, d = x_prompt.shape
    nb = x_sample.shape[0]
    assert bp == 1 and x_sample.shape[1] == 1
    dc = conv_w.shape[2]
    da = w_out.shape[1] - dc
    hd = da // N_HEADS
    dk = hd // 2

    xp = x_prompt.reshape(seq, d)
    xs = x_sample.reshape(nb, d)
    pad = (-(bp + nb)) % 8
    c_all = jnp.concatenate([c_prompt, c_sample, jnp.zeros((pad, d), F32)], axis=0)
    row = lambda a: a.reshape(1, -1)

    kp_l, vp_l, cp_l, ks_l, vs_l, cs_l = [], [], [], [], [], []
    for l in range(depth):
        w_in_b = w_in[l].astype(BF16)
        w_out_b = w_out[l].astype(BF16)
        mod = _modulation(c_all, w_ada[l], row(b_ada[l]))
        shift, scale, gate = mod[:, 0:d], mod[:, d:2 * d], mod[:, 2 * d:3 * d]
        vecs = (row(conv_b[l]), row(g_cn[l]), row(b_cn[l]))
        lams = (row(lq1[l]), row(lk1[l]), row(lq2[l]), row(lk2[l]))
        g_sub = row(g_subln[l])
        conv_w8 = jnp.broadcast_to(conv_w[l][:, None, :], (CONV_WIDTH, SUBLANES, dc))

        sl = slice(bp, bp + nb)
        oc, u_tail, q, k, v, ga, proj_s = _in_proj(
            xp, shift[0:1], scale[0:1], xs, shift[sl], scale[sl], row(g_pre[l]), w_in_b, conv_w8, *vecs,
            dc=dc, da=da, q_scale=dk ** -0.5 * math.log2(math.e))
        att_col = 3 * dc
        ksn = proj_s[:, att_col + da:att_col + 2 * da]
        vsn = proj_s[:, att_col + 2 * da:att_col + 3 * da]

        ocs, new_state = _conv_sample(state_conv[l], proj_s, conv_w[l], *vecs)

        oa, oas = _attention(q, k, v, ga, g_sub, *lams, page_table, proj_s, att_col,
                             cache_k[l], cache_v[l], layer_idx=l)

        xp = _out_proj(oc, oa, w_out_b, xp, gate[0:1], row(g_post[l]), bm=512)
        xs = _out_proj(ocs, oas.reshape(nb, da), w_out_b, xs, gate[sl], row(g_post[l]), bm=nb)
        kp_l.append(k.reshape(bp, seq, N_HEADS, hd))
        vp_l.append(v.reshape(bp, seq, N_HEADS, hd))
        cp_l.append(u_tail[CONV_HALO - CONV_STATE:].reshape(bp, CONV_STATE, dc))
        ks_l.append(ksn.reshape(nb, 1, N_HEADS, hd))
        vs_l.append(vsn.reshape(nb, 1, N_HEADS, hd))
        cs_l.append(new_state)

    return (xp.reshape(bp, seq, d), xs.reshape(nb, 1, d),
            jnp.stack(kp_l), jnp.stack(vp_l), jnp.stack(cp_l),
            jnp.stack(ks_l), jnp.stack(vs_l), jnp.stack(cs_l))
```

```python
import functools
import math

import jax
import jax.numpy as jnp
from jax import lax
from jax.experimental import pallas as pl
from jax.experimental.pallas import tpu as pltpu

F32 = jnp.float32
BF16 = jnp.bfloat16

N_HEADS = 8
CONV_WIDTH = 31
CONV_STATE = CONV_WIDTH - 1
CONV_HALO = 32
EPS = 1e-6

LANES = 128
SUBLANES = 8
MXU_TILE = 256
VMEM_LIMIT = 56 * 1024 * 1024

NT_DIMS = (((1,), (1,)), ((), ()))


def _lambda_init(layer_idx):
    return 0.8 - 0.6 * math.exp(-0.3 * layer_idx)


def _params(*sem):
    return pltpu.CompilerParams(dimension_semantics=sem, vmem_limit_bytes=VMEM_LIMIT)


def _sigmoid(x):
    return 1.0 / (1.0 + jnp.exp(-x))


def _silu(x):
    return x * _sigmoid(x)


def _mod_kernel(c_ref, w_ref, b_ref, o_ref):
    a = _silu(c_ref[...]).astype(BF16)
    o_ref[...] = jnp.dot(a, w_ref[...].astype(BF16), preferred_element_type=F32) + b_ref[...]


def _modulation(c_all, w_ada, b_ada, bn=512):
    rows, d = c_all.shape
    n = w_ada.shape[1]
    return pl.pallas_call(
        _mod_kernel,
        out_shape=jax.ShapeDtypeStruct((rows, n), F32),
        grid=(n // bn,),
        in_specs=[pl.BlockSpec((rows, d), lambda j: (0, 0)),
                  pl.BlockSpec((d, bn), lambda j: (0, j)),
                  pl.BlockSpec((1, bn), lambda j: (0, j))],
        out_specs=pl.BlockSpec((rows, bn), lambda j: (0, j)),
        compiler_params=_params("arbitrary"),
        name="adaln_mod",
    )(c_all, w_ada, b_ada)


def _ln_swish_gate(conv, g_ref, b_ref, gate):
    mu = jnp.mean(conv, axis=-1, keepdims=True)
    d = conv - mu
    var = jnp.mean(d * d, axis=-1, keepdims=True)
    y = d * lax.rsqrt(var + EPS) * g_ref[...] + b_ref[...]
    return _silu(y) * gate


def _causal_conv_rows(ext_ref, sh_ref, conv_ref, w_ref, cb_ref, row0, bt, rows=32):
    span = sh_ref.shape[1]
    for b in range(1, SUBLANES):
        sh_ref[b - 1] = ext_ref[row0 + b:row0 + b + span, :]
    off = CONV_HALO - CONV_STATE
    reps = rows // SUBLANES
    for c in range(ext_ref.shape[1] // LANES):
        cs = slice(c * LANES, (c + 1) * LANES)
        taps = [jnp.concatenate([w_ref[j, :, cs]] * reps, axis=0) for j in range(CONV_WIDTH)]
        bias = jnp.broadcast_to(cb_ref[:, cs], (rows, LANES))
        for r in range(bt // rows):
            acc = bias
            for j in range(CONV_WIDTH):
                a, b = divmod(off + j, SUBLANES)
                base = r * rows + SUBLANES * a
                if b == 0:
                    src = ext_ref[row0 + base:row0 + base + rows, cs]
                else:
                    src = sh_ref[b - 1, base:base + rows, cs]
                acc = acc + taps[j] * src
            conv_ref[row0 + r * rows:row0 + (r + 1) * rows, cs] = acc


def _modulated_norm(x_ref, shift_ref, scale_ref, g_ref):
    x = x_ref[...]
    y = x * lax.rsqrt(jnp.mean(x * x, axis=-1, keepdims=True) + EPS) * g_ref[...]
    return (y * (1.0 + scale_ref[...]) + shift_ref[...]).astype(BF16)


def _proj_kernel(x_ref, shift_ref, scale_ref, xr_ref, shiftr_ref, scaler_ref, g_ref,
                 wa_ref, wb_ref, wzc_ref, wq_ref, wk_ref, wv_ref, wza_ref,
                 cw_ref, cb_ref, gcn_ref, bcn_ref,
                 oc_ref, utail_ref, q_ref, k_ref, v_ref, ga_ref, projr_ref, ext_ref, sh_ref, conv_ref,
                 *, q_scale):
    step = pl.program_id(0)
    last = pl.num_programs(0) - 1
    w_refs = (wa_ref, wb_ref, wzc_ref, wq_ref, wk_ref, wv_ref, wza_ref)

    @pl.when(step < last)
    def _():
        h = _modulated_norm(x_ref, shift_ref, scale_ref, g_ref)
        bm = h.shape[0]
        dc = oc_ref.shape[1]

        def seg(w_ref):
            return jnp.dot(h, w_ref[...], preferred_element_type=F32)

        u = seg(wa_ref) * _sigmoid(seg(wb_ref))
        gate_c = _silu(seg(wzc_ref))

        @pl.when(step == 0)
        def _():
            ext_ref[0:CONV_HALO, :] = jnp.zeros((CONV_HALO, dc), F32)

        ext_ref[CONV_HALO:CONV_HALO + bm, :] = u
        utail_ref[...] = u[bm - CONV_HALO:bm, :]
        bt = sh_ref.shape[1] + SUBLANES - CONV_HALO
        for row0 in range(0, bm, bt):
            _causal_conv_rows(ext_ref, sh_ref, conv_ref, cw_ref, cb_ref, row0, bt)
        oc_ref[...] = _ln_swish_gate(conv_ref[...], gcn_ref, bcn_ref, gate_c).astype(oc_ref.dtype)
        ext_ref[0:CONV_HALO, :] = ext_ref[bm:bm + CONV_HALO, :]

        q_ref[...] = (seg(wq_ref) * q_scale).astype(q_ref.dtype)
        k_ref[...] = seg(wk_ref)
        v_ref[...] = seg(wv_ref)
        ga_ref[...] = _silu(seg(wza_ref)).astype(ga_ref.dtype)

    @pl.when(step == last)
    def _():
        hr = _modulated_norm(xr_ref, shiftr_ref, scaler_ref, g_ref)
        start = 0
        for w_ref in w_refs:
            width = w_ref.shape[1]
            projr_ref[:, start:start + width] = jnp.dot(hr, w_ref[...], preferred_element_type=F32)
            start += width


def _in_proj(x, shift, scale, xr, shift_r, scale_r, g_pre, w_in_bf16, conv_w8, conv_b, g_cn, b_cn,
             dc, da, q_scale, bm=256, conv_bt=128):
    m, d = x.shape
    assert bm % conv_bt == 0 and shift.shape[0] == 1
    n_tiles = m // bm
    kern = functools.partial(_proj_kernel, q_scale=q_scale)
    row = lambda w: pl.BlockSpec((bm, w), lambda i: (jnp.minimum(i, n_tiles - 1), 0))
    const = lambda a: pl.BlockSpec(a.shape, lambda i: (0,) * a.ndim)
    segments = [(t * dc, dc) for t in range(3)] + [(3 * dc + t * da, da) for t in range(4)]
    assert all(start % width == 0 for start, width in segments) and segments[-1][0] + da == w_in_bf16.shape[1]
    w_specs = [pl.BlockSpec((d, width), lambda i, blk=start // width: (0, blk), pipeline_mode=pl.Buffered(1))
               for start, width in segments]
    n = w_in_bf16.shape[1]
    return pl.pallas_call(
        kern,
        out_shape=(jax.ShapeDtypeStruct((m, dc), BF16), jax.ShapeDtypeStruct((CONV_HALO, dc), F32),
                   jax.ShapeDtypeStruct((m, da), BF16), jax.ShapeDtypeStruct((m, da), F32),
                   jax.ShapeDtypeStruct((m, da), F32), jax.ShapeDtypeStruct((m, da), BF16),
                   jax.ShapeDtypeStruct((xr.shape[0], n), F32)),
        grid=(n_tiles + 1,),
        in_specs=[row(d), const(shift), const(scale), const(xr), const(shift_r), const(scale_r), const(g_pre),
                  *w_specs, const(conv_w8), const(conv_b), const(g_cn), const(b_cn)],
        out_specs=(row(dc), pl.BlockSpec((CONV_HALO, dc), lambda i: (0, 0)),
                   row(da), row(da), row(da), row(da),
                   pl.BlockSpec((xr.shape[0], n), lambda i: (0, 0))),
        scratch_shapes=[pltpu.VMEM((bm + CONV_HALO, dc), F32),
                        pltpu.VMEM((SUBLANES - 1, conv_bt + CONV_HALO - SUBLANES, dc), F32),
                        pltpu.VMEM((bm, dc), F32)],
        compiler_params=_params("arbitrary"),
        name="in_proj",
    )(x, shift, scale, xr, shift_r, scale_r, g_pre, *([w_in_bf16] * len(segments)), conv_w8, conv_b, g_cn, b_cn)


def _conv_sample_kernel(state_ref, a_ref, b_ref, z_ref, w_ref, cb_ref, gcn_ref, bcn_ref,
                        o_ref, ns_ref, conv_ref):
    nb = state_ref.shape[0]
    w_hist = w_ref[0:CONV_STATE, :]
    u = a_ref[...] * _sigmoid(b_ref[...])
    for b in range(nb):
        st = state_ref[b]
        conv_ref[b:b + 1, :] = jnp.sum(st * w_hist, axis=0, keepdims=True)
        ns_ref[b, 0:CONV_STATE - 1, :] = state_ref[b, 1:CONV_STATE, :]
        ns_ref[b, CONV_STATE - 1:CONV_STATE, :] = u[b:b + 1, :]
    conv = conv_ref[...] + w_ref[CONV_STATE:CONV_WIDTH, :] * u + cb_ref[...]
    o_ref[...] = _ln_swish_gate(conv, gcn_ref, bcn_ref, _silu(z_ref[...])).astype(o_ref.dtype)


def _conv_sample(state, proj, conv_w, conv_b, g_cn, b_cn):
    nb, _, dc = state.shape
    const = lambda a: pl.BlockSpec(a.shape, lambda i: (0,) * a.ndim)
    col = lambda c: pl.BlockSpec((nb, dc), lambda i: (0, c))
    return pl.pallas_call(
        _conv_sample_kernel,
        out_shape=(jax.ShapeDtypeStruct((nb, dc), BF16),
                   jax.ShapeDtypeStruct((nb, CONV_STATE, dc), F32)),
        grid=(1,),
        in_specs=[const(state), col(0), col(1), col(2), const(conv_w), const(conv_b), const(g_cn), const(b_cn)],
        out_specs=(pl.BlockSpec((nb, dc), lambda i: (0, 0)),
                   pl.BlockSpec((nb, CONV_STATE, dc), lambda i: (0, 0, 0))),
        scratch_shapes=[pltpu.VMEM((nb, dc), F32)],
        compiler_params=_params("arbitrary"),
        name="conv_sample",
    )(state, proj, proj, proj, conv_w, conv_b, g_cn, b_cn)


def _lambda_full(lq1_ref, lk1_ref, lq2_ref, lk2_ref, layer_idx):
    s1 = jnp.sum(lq1_ref[...] * lk1_ref[...], axis=-1, keepdims=True)
    s2 = jnp.sum(lq2_ref[...] * lk2_ref[...], axis=-1, keepdims=True)
    return jnp.exp(s1) - jnp.exp(s2) + _lambda_init(layer_idx)


def _subln_gate(o, g_ref, gate, layer_idx):
    y = o * lax.rsqrt(jnp.mean(o * o, axis=-1, keepdims=True) + EPS) * g_ref[...]
    return y * (1.0 - _lambda_init(layer_idx)) * gate


def _trips_before(i):
    return (i * i) // 4


def _attn_kernel(pt_ref,
                 q_ref, k_ref, v_ref, ga_ref, g_ref, lq1_ref, lk1_ref, lq2_ref, lk2_ref,
                 qd_ref, knd_ref, vnd_ref, zd_ref, ck_hbm, cv_hbm,
                 o_ref, od_ref,
                 kb_ref, vt_ref, qs_ref, sa_ref, sb_ref, mxa_ref, mxb_ref, m_ref, l_ref, acc_ref,
                 kpg_ref, vpg_ref, ksem, vsem, md_ref, ld_ref, accd_ref,
                 *, bq, layer_idx, pages, trips_per_seq, n_trips):
    h = pl.program_id(0)
    i = pl.program_id(1)
    n_chunks = kb_ref.shape[0]
    hd = q_ref.shape[1]
    half = hd // 2
    nh = qd_ref.shape[1]
    n_slots = kpg_ref.shape[0]
    trips_per_head = _trips_before(n_chunks)

    def page_copies(trip, slot, from_table):
        copies = []
        for j in range(pages):
            page = pt_ref[trip * pages + j] if from_table else 0
            copies.append(pltpu.make_async_copy(ck_hbm.at[page], kpg_ref.at[slot, j], ksem.at[slot]))
            copies.append(pltpu.make_async_copy(cv_hbm.at[page], vpg_ref.at[slot, j], vsem.at[slot]))
        return copies

    @pl.when(jnp.logical_and(h == 0, i == 0))
    def _():
        for trip in range(n_slots - 1):
            for cp in page_copies(trip, trip, True):
                cp.start()

    @pl.when(i == 0)
    def _():
        for c in range(n_chunks):
            kb_ref[c] = k_ref[c * bq:(c + 1) * bq, :].astype(BF16)
            vt_ref[c] = v_ref[c * bq:(c + 1) * bq, :].T.astype(BF16)

    q_t = q_ref[...].astype(F32).T
    d_idx = lax.broadcasted_iota(jnp.int32, q_t.shape, 0)
    qs_ref[:, 0:bq] = jnp.where(d_idx < half, q_t, 0.0).astype(BF16)
    qs_ref[:, bq:2 * bq] = jnp.where(d_idx >= half, q_t, 0.0).astype(BF16)
    m_ref[...] = jnp.full(m_ref.shape, -jnp.inf, F32)
    l_ref[...] = jnp.zeros(l_ref.shape, F32)
    acc_ref[...] = jnp.zeros(acc_ref.shape, F32)

    tile_w = MXU_TILE
    n_tiles = 2 * bq // tile_w

    def scores_tile(c, s_ref, mx_ref, t, diagonal=False):
        cs = slice(t * tile_w, (t + 1) * tile_w)
        s = jnp.dot(kb_ref[c], qs_ref[:, cs], preferred_element_type=F32)
        if diagonal:
            kidx = lax.broadcasted_iota(jnp.int32, s.shape, 0)
            qidx = (lax.broadcasted_iota(jnp.int32, s.shape, 1) + t * tile_w) & (bq - 1)
            s = jnp.where(kidx <= qidx, s, -jnp.inf)
        s_ref[:, cs] = s
        mx_ref[:, cs] = jnp.max(s, axis=0, keepdims=True)

    def accumulate_tile(c, s_ref, mx_ref, t):
        cs = slice(t * tile_w, (t + 1) * tile_w)
        m_old = m_ref[:, cs]
        m_new = jnp.maximum(m_old, mx_ref[:, cs])
        p = jnp.exp2(s_ref[:, cs] - m_new)
        alpha = jnp.exp2(m_old - m_new)
        l_ref[:, cs] = alpha * l_ref[:, cs] + jnp.sum(p, axis=0, keepdims=True)
        pv = jnp.dot(vt_ref[c], p.astype(BF16), preferred_element_type=F32)
        acc_ref[:, cs] = alpha * acc_ref[:, cs] + pv
        m_ref[:, cs] = m_new

    def decode_query(b):
        q = qd_ref[b] * half ** -0.5
        lane = lax.broadcasted_iota(jnp.int32, q.shape, 1)
        qx = jnp.concatenate([jnp.where(lane < half, q, 0.0), jnp.where(lane >= half, q, 0.0)], axis=0)
        return qx.astype(BF16)

    def decode_scores(b, slot):
        rows = pages * kpg_ref.shape[2]
        col_head = lax.broadcasted_iota(jnp.int32, (2 * nh, rows), 1) & (nh - 1)
        row_head = lax.broadcasted_iota(jnp.int32, (2 * nh, rows), 0) & (nh - 1)
        k_all = kpg_ref[slot].reshape(rows, hd).astype(BF16)
        s = lax.dot_general(decode_query(b), k_all, NT_DIMS, preferred_element_type=F32)
        return jnp.where(col_head == row_head, s, -jnp.inf)

    def decode_update(s, slot, first):
        m_old = jnp.where(first, -jnp.inf, md_ref[...])
        l_old = jnp.where(first, 0.0, ld_ref[...])
        acc_old = jnp.where(first, 0.0, accd_ref[...])
        m_new = jnp.maximum(m_old, jnp.max(s, axis=-1, keepdims=True))
        alpha = jnp.exp(m_old - m_new)
        p = jnp.exp(s - m_new)
        v_all = vpg_ref[slot].reshape(s.shape[1], hd).astype(BF16)
        md_ref[...] = m_new
        ld_ref[...] = alpha * l_old + jnp.sum(p, axis=-1, keepdims=True)
        accd_ref[...] = alpha * acc_old + jnp.dot(p.astype(BF16), v_all, preferred_element_type=F32)

    def decode_finish(b, lam):
        qx = decode_query(b).astype(F32)
        kn = knd_ref[b].astype(BF16).astype(F32)
        vn = vnd_ref[b].astype(BF16).astype(F32)
        kn2 = jnp.concatenate([kn, kn], axis=0)
        vn2 = jnp.concatenate([vn, vn], axis=0)
        s_new = jnp.sum(qx * kn2, axis=-1, keepdims=True)
        m_old = md_ref[...]
        m_fin = jnp.maximum(m_old, s_new)
        a_fin = jnp.exp(m_old - m_fin)
        p_new = jnp.exp(s_new - m_fin)
        l_fin = a_fin * ld_ref[...] + p_new
        on = (a_fin * accd_ref[...] + p_new * vn2) / l_fin
        o = on[0:nh] - lam * on[nh:2 * nh]
        od_ref[b] = _subln_gate(o, g_ref, _silu(zd_ref[b]), layer_idx)

    for t in range(n_tiles):
        scores_tile(i, sa_ref, mxa_ref, t, diagonal=True)
    trip0 = h * trips_per_head + _trips_before(i)

    def body(p, carry):
        trip = trip0 + p
        slot = lax.rem(trip, n_slots)
        for cp in page_copies(trip, slot, False):
            cp.wait()

        @pl.when(trip + n_slots - 1 < n_trips)
        def _():
            nxt = trip + n_slots - 1
            for cp in page_copies(nxt, lax.rem(nxt, n_slots), True):
                cp.start()

        seq = trip // trips_per_seq
        part = lax.rem(trip, trips_per_seq)
        c_cur = jnp.where(p == 0, i, 2 * p - 1)
        for t in range(n_tiles):
            scores_tile(2 * p, sb_ref, mxb_ref, t)
            accumulate_tile(c_cur, sa_ref, mxa_ref, t)
        s_dec = decode_scores(seq, slot)
        c_next = jnp.minimum(2 * p + 1, i - 1)
        for t in range(n_tiles):
            scores_tile(c_next, sa_ref, mxa_ref, t)
            accumulate_tile(2 * p, sb_ref, mxb_ref, t)
        decode_update(s_dec, slot, part == 0)

        @pl.when(part == trips_per_seq - 1)
        def _():
            decode_finish(seq, _lambda_full(lq1_ref, lk1_ref, lq2_ref, lk2_ref, layer_idx))

        return carry

    lax.fori_loop(0, (i + 1) // 2, body, 0)

    @pl.when(i % 2 == 0)
    def _():
        for t in range(n_tiles):
            accumulate_tile(jnp.maximum(i - 1, 0), sa_ref, mxa_ref, t)

    lam = _lambda_full(lq1_ref, lk1_ref, lq2_ref, lk2_ref, layer_idx)
    acc = acc_ref[...]
    inv_l = 1.0 / l_ref[...]
    o_t = acc[:, 0:bq] * inv_l[:, 0:bq] - lam * (acc[:, bq:2 * bq] * inv_l[:, bq:2 * bq])
    o = o_t.T
    o_ref[...] = _subln_gate(o, g_ref, ga_ref[...].astype(F32), layer_idx).astype(o_ref.dtype)


def _attention(q, k, v, ga, g_subln, lq1, lk1, lq2, lk2,
               page_table, proj_d, att_col, cache_k, cache_v, layer_idx, bq=512, n_slots=5):
    s, da = q.shape
    hd = da // N_HEADS
    nq = s // bq
    nb, n_pages = page_table.shape
    n_pool, page, nh, _ = cache_k.shape
    ck = cache_k.reshape(n_pool, page * nh, hd)
    cv = cache_v.reshape(n_pool, page * nh, hd)
    pt = page_table.reshape(nb * n_pages)
    assert att_col % da == 0
    proj_heads = proj_d.reshape(nb, proj_d.shape[1] // hd, hd)
    first_blk = att_col // da

    n_trips = N_HEADS * _trips_before(nq)
    pages = (nb * n_pages) // n_trips
    assert pages * n_trips == nb * n_pages and n_pages % pages == 0, (n_trips, nb, n_pages)
    assert n_trips >= n_slots
    trips_per_seq = n_pages // pages

    kern = functools.partial(_attn_kernel, bq=bq, layer_idx=layer_idx, pages=pages,
                             trips_per_seq=trips_per_seq, n_trips=n_trips)
    tile = pl.BlockSpec((bq, hd), lambda h, i, pt: (i, h))
    full = pl.BlockSpec((s, hd), lambda h, i, pt: (0, h))
    small = lambda a: pl.BlockSpec(a.shape, lambda h, i, pt: (0, 0))
    whole = pl.BlockSpec((nb, nh, hd), lambda h, i, pt: (0, 0, 0))
    dec = lambda blk: pl.BlockSpec((nb, nh, hd), lambda h, i, pt: (0, first_blk + blk, 0))
    hbm = pl.BlockSpec(memory_space=pl.ANY)
    grid_spec = pltpu.PrefetchScalarGridSpec(
        num_scalar_prefetch=1,
        grid=(N_HEADS, nq),
        in_specs=[tile, full, full, tile, small(g_subln), small(lq1), small(lk1), small(lq2), small(lk2),
                  dec(0), dec(1), dec(2), dec(3), hbm, hbm],
        out_specs=(tile, whole),
        scratch_shapes=[pltpu.VMEM((nq, bq, hd), BF16),
                        pltpu.VMEM((nq, hd, bq), BF16),
                        pltpu.VMEM((hd, 2 * bq), BF16),
                        pltpu.VMEM((bq, 2 * bq), F32), pltpu.VMEM((bq, 2 * bq), F32),
                        pltpu.VMEM((1, 2 * bq), F32), pltpu.VMEM((1, 2 * bq), F32),
                        pltpu.VMEM((1, 2 * bq), F32), pltpu.VMEM((1, 2 * bq), F32),
                        pltpu.VMEM((hd, 2 * bq), F32),
                        pltpu.VMEM((n_slots, pages, page * nh, hd), F32),
                        pltpu.VMEM((n_slots, pages, page * nh, hd), F32),
                        pltpu.SemaphoreType.DMA((n_slots,)), pltpu.SemaphoreType.DMA((n_slots,)),
                        pltpu.VMEM((2 * nh, 1), F32), pltpu.VMEM((2 * nh, 1), F32),
                        pltpu.VMEM((2 * nh, hd), F32)],
    )
    return pl.pallas_call(
        kern,
        out_shape=(jax.ShapeDtypeStruct((s, da), BF16), jax.ShapeDtypeStruct((nb, nh, hd), F32)),
        grid_spec=grid_spec,
        compiler_params=_params("arbitrary", "arbitrary"),
        name="attention",
    )(pt, q, k, v, ga, g_subln, lq1, lk1, lq2, lk2,
      proj_heads, proj_heads, proj_heads, proj_heads, ck, cv)


def _out_kernel(oc_ref, oa_ref, wc_ref, wa_ref, x_ref, gate_ref, g_ref, o_ref):
    y = jnp.dot(oc_ref[...].astype(BF16), wc_ref[...], preferred_element_type=F32)
    y = y + jnp.dot(oa_ref[...].astype(BF16), wa_ref[...], preferred_element_type=F32)
    yn = y * lax.rsqrt(jnp.mean(y * y, axis=-1, keepdims=True) + EPS) * g_ref[...]
    o_ref[...] = x_ref[...] + gate_ref[...] * yn


def _out_proj(oc, oa, w_out_bf16, x, gate, g_post, bm):
    m, d = x.shape
    dc = oc.shape[1]
    da = oa.shape[1]
    per_row = gate.shape[0] != 1
    assert dc % da == 0
    gate_spec = pl.BlockSpec((bm, d), lambda i: (i, 0)) if per_row else pl.BlockSpec((1, d), lambda i: (0, 0))
    return pl.pallas_call(
        _out_kernel,
        out_shape=jax.ShapeDtypeStruct((m, d), F32),
        grid=(m // bm,),
        in_specs=[pl.BlockSpec((bm, dc), lambda i: (i, 0)),
                  pl.BlockSpec((bm, da), lambda i: (i, 0)),
                  pl.BlockSpec((dc, d), lambda i: (0, 0)),
                  pl.BlockSpec((da, d), lambda i: (dc // da, 0)),
                  pl.BlockSpec((bm, d), lambda i: (i, 0)),
                  gate_spec,
                  pl.BlockSpec((1, d), lambda i: (0, 0))],
        out_specs=pl.BlockSpec((bm, d), lambda i: (i, 0)),
        compiler_params=_params("arbitrary"),
        name="out_proj",
    )(oc, oa, w_out_bf16, w_out_bf16, x, gate, g_post)


def kernel(x_prompt, x_sample, cache_k, cache_v, state_conv, page_table, c_prompt, c_sample,
           w_ada, b_ada, g_pre, g_post, w_in, conv_w, conv_b, g_cn, b_cn,
           lq1, lk1, lq2, lk2, g_subln, w_out):
    depth = w_ada.shape[0]
    bp, seq, d = x_prompt.shape
    nb = x_sample.shape[0]
    assert bp == 1 and x_sample.shape[1] == 1
    dc = conv_w.shape[2]
    da = w_out.shape[1] - dc
    hd = da // N_HEADS
    dk = hd // 2

    xp = x_prompt.reshape(seq, d)
    xs = x_sample.reshape(nb, d)
    pad = (-(bp + nb)) % 8
    c_all = jnp.concatenate([c_prompt, c_sample, jnp.zeros((pad, d), F32)], axis=0)
    row = lambda a: a.reshape(1, -1)

    kp_l, vp_l, cp_l, ks_l, vs_l, cs_l = [], [], [], [], [], []
    for l in range(depth):
        w_in_b = w_in[l].astype(BF16)
        w_out_b = w_out[l].astype(BF16)
        mod = _modulation(c_all, w_ada[l], row(b_ada[l]))
        shift, scale, gate = mod[:, 0:d], mod[:, d:2 * d], mod[:, 2 * d:3 * d]
        vecs = (row(conv_b[l]), row(g_cn[l]), row(b_cn[l]))
        lams = (row(lq1[l]), row(lk1[l]), row(lq2[l]), row(lk2[l]))
        g_sub = row(g_subln[l])
        conv_w8 = jnp.broadcast_to(conv_w[l][:, None, :], (CONV_WIDTH, SUBLANES, dc))

        sl = slice(bp, bp + nb)
        oc, u_tail, q, k, v, ga, proj_s = _in_proj(
            xp, shift[0:1], scale[0:1], xs, shift[sl], scale[sl], row(g_pre[l]), w_in_b, conv_w8, *vecs,
            dc=dc, da=da, q_scale=dk ** -0.5 * math.log2(math.e))
        att_col = 3 * dc
        ksn = proj_s[:, att_col + da:att_col + 2 * da]
        vsn = proj_s[:, att_col + 2 * da:att_col + 3 * da]

        ocs, new_state = _conv_sample(state_conv[l], proj_s, conv_w[l], *vecs)

        oa, oas = _attention(q, k, v, ga, g_sub, *lams, page_table, proj_s, att_col,
                             cache_k[l], cache_v[l], layer_idx=l)

        xp = _out_proj(oc, oa, w_out_b, xp, gate[0:1], row(g_post[l]), bm=512)
        xs = _out_proj(ocs, oas.reshape(nb, da), w_out_b, xs, gate[sl], row(g_post[l]), bm=nb)
        kp_l.append(k.reshape(bp, seq, N_HEADS, hd))
        vp_l.append(v.reshape(bp, seq, N_HEADS, hd))
        cp_l.append(u_tail[CONV_HALO - CONV_STATE:].reshape(bp, CONV_STATE, dc))
        ks_l.append(ksn.reshape(nb, 1, N_HEADS, hd))
        vs_l.append(vsn.reshape(nb, 1, N_HEADS, hd))
        cs_l.append(new_state)

    return (xp.reshape(bp, seq, d), xs.reshape(nb, 1, d),
            jnp.stack(kp_l), jnp.stack(vp_l), jnp.stack(cp_l),
            jnp.stack(ks_l), jnp.stack(vs_l), jnp.stack(cs_l))
```

```python
import functools
import math

import jax
import jax.numpy as jnp
from jax import lax
from jax.experimental import pallas as pl
from jax.experimental.pallas import tpu as pltpu

F32 = jnp.float32
BF16 = jnp.bfloat16

N_HEADS = 8
CONV_WIDTH = 31
CONV_STATE = CONV_WIDTH - 1
CONV_HALO = 32
EPS = 1e-6

LANES = 128
SUBLANES = 8
MXU_TILE = 256
VMEM_LIMIT = 56 * 1024 * 1024

NT_DIMS = (((1,), (1,)), ((), ()))


def _lambda_init(layer_idx):
    return 0.8 - 0.6 * math.exp(-0.3 * layer_idx)


def _params(*sem):
    return pltpu.CompilerParams(dimension_semantics=sem, vmem_limit_bytes=VMEM_LIMIT)


def _sigmoid(x):
    return 1.0 / (1.0 + jnp.exp(-x))


def _silu(x):
    return x * _sigmoid(x)


def _mod_kernel(c_ref, w_ref, b_ref, o_ref):
    a = _silu(c_ref[...]).astype(BF16)
    o_ref[...] = jnp.dot(a, w_ref[...].astype(BF16), preferred_element_type=F32) + b_ref[...]


def _modulation(c_all, w_ada, b_ada, bn=512):
    rows, d = c_all.shape
    n = w_ada.shape[1]
    return pl.pallas_call(
        _mod_kernel,
        out_shape=jax.ShapeDtypeStruct((rows, n), F32),
        grid=(n // bn,),
        in_specs=[pl.BlockSpec((rows, d), lambda j: (0, 0)),
                  pl.BlockSpec((d, bn), lambda j: (0, j)),
                  pl.BlockSpec((1, bn), lambda j: (0, j))],
        out_specs=pl.BlockSpec((rows, bn), lambda j: (0, j)),
        compiler_params=_params("arbitrary"),
        name="adaln_mod",
    )(c_all, w_ada, b_ada)


def _ln_swish_gate(conv, g_ref, b_ref, gate):
    mu = jnp.mean(conv, axis=-1, keepdims=True)
    d = conv - mu
    var = jnp.mean(d * d, axis=-1, keepdims=True)
    y = d * lax.rsqrt(var + EPS) * g_ref[...] + b_ref[...]
    return _silu(y) * gate


def _causal_conv_rows(ext_ref, sh_ref, conv_ref, w_ref, cb_ref, row0, bt, rows=32):
    span = sh_ref.shape[1]
    for b in range(1, SUBLANES):
        sh_ref[b - 1] = ext_ref[row0 + b:row0 + b + span, :]
    off = CONV_HALO - CONV_STATE
    reps = rows // SUBLANES
    for c in range(ext_ref.shape[1] // LANES):
        cs = slice(c * LANES, (c + 1) * LANES)
        taps = [jnp.concatenate([w_ref[j, :, cs]] * reps, axis=0) for j in range(CONV_WIDTH)]
        bias = jnp.broadcast_to(cb_ref[:, cs], (rows, LANES))
        for r in range(bt // rows):
            acc = bias
            for j in range(CONV_WIDTH):
                a, b = divmod(off + j, SUBLANES)
                base = r * rows + SUBLANES * a
                if b == 0:
                    src = ext_ref[row0 + base:row0 + base + rows, cs]
                else:
                    src = sh_ref[b - 1, base:base + rows, cs]
                acc = acc + taps[j] * src
            conv_ref[row0 + r * rows:row0 + (r + 1) * rows, cs] = acc


def _modulated_norm(x_ref, shift_ref, scale_ref, g_ref):
    x = x_ref[...]
    y = x * lax.rsqrt(jnp.mean(x * x, axis=-1, keepdims=True) + EPS) * g_ref[...]
    return (y * (1.0 + scale_ref[...]) + shift_ref[...]).astype(BF16)


def _proj_kernel(x_ref, shift_ref, scale_ref, xr_ref, shiftr_ref, scaler_ref, g_ref,
                 wa_ref, wb_ref, wzc_ref, wq_ref, wk_ref, wv_ref, wza_ref,
                 cw_ref, cb_ref, gcn_ref, bcn_ref,
                 oc_ref, utail_ref, q_ref, k_ref, v_ref, ga_ref, projr_ref, ext_ref, sh_ref, conv_ref,
                 *, q_scale):
    step = pl.program_id(0)
    last = pl.num_programs(0) - 1
    w_refs = (wa_ref, wb_ref, wzc_ref, wq_ref, wk_ref, wv_ref, wza_ref)

    @pl.when(step < last)
    def _():
        h = _modulated_norm(x_ref, shift_ref, scale_ref, g_ref)
        bm = h.shape[0]
        dc = oc_ref.shape[1]

        def seg(w_ref):
            return jnp.dot(h, w_ref[...], preferred_element_type=F32)

        u = seg(wa_ref) * _sigmoid(seg(wb_ref))
        gate_c = _silu(seg(wzc_ref))

        @pl.when(step == 0)
        def _():
            ext_ref[0:CONV_HALO, :] = jnp.zeros((CONV_HALO, dc), F32)

        ext_ref[CONV_HALO:CONV_HALO + bm, :] = u
        utail_ref[...] = u[bm - CONV_HALO:bm, :]
        bt = sh_ref.shape[1] + SUBLANES - CONV_HALO
        for row0 in range(0, bm, bt):
            _causal_conv_rows(ext_ref, sh_ref, conv_ref, cw_ref, cb_ref, row0, bt)
        oc_ref[...] = _ln_swish_gate(conv_ref[...], gcn_ref, bcn_ref, gate_c).astype(oc_ref.dtype)
        ext_ref[0:CONV_HALO, :] = ext_ref[bm:bm + CONV_HALO, :]

        q_ref[...] = (seg(wq_ref) * q_scale).astype(q_ref.dtype)
        k_ref[...] = seg(wk_ref)
        v_ref[...] = seg(wv_ref)
        ga_ref[...] = _silu(seg(wza_ref)).astype(ga_ref.dtype)

    @pl.when(step == last)
    def _():
        hr = _modulated_norm(xr_ref, shiftr_ref, scaler_ref, g_ref)
        start = 0
        for w_ref in w_refs:
            width = w_ref.shape[1]
            projr_ref[:, start:start + width] = jnp.dot(hr, w_ref[...], preferred_element_type=F32)
            start += width


def _in_proj(x, shift, scale, xr, shift_r, scale_r, g_pre, w_in_bf16, conv_w8, conv_b, g_cn, b_cn,
             dc, da, q_scale, bm=256, conv_bt=128):
    m, d = x.shape
    assert bm % conv_bt == 0 and shift.shape[0] == 1
    n_tiles = m // bm
    kern = functools.partial(_proj_kernel, q_scale=q_scale)
    row = lambda w: pl.BlockSpec((bm, w), lambda i: (jnp.minimum(i, n_tiles - 1), 0))
    const = lambda a: pl.BlockSpec(a.shape, lambda i: (0,) * a.ndim)
    segments = [(t * dc, dc) for t in range(3)] + [(3 * dc + t * da, da) for t in range(4)]
    assert all(start % width == 0 for start, width in segments) and segments[-1][0] + da == w_in_bf16.shape[1]
    w_specs = [pl.BlockSpec((d, width), lambda i, blk=start // width: (0, blk), pipeline_mode=pl.Buffered(1))
               for start, width in segments]
    n = w_in_bf16.shape[1]
    return pl.pallas_call(
        kern,
        out_shape=(jax.ShapeDtypeStruct((m, dc), BF16), jax.ShapeDtypeStruct((CONV_HALO, dc), F32),
                   jax.ShapeDtypeStruct((m, da), BF16), jax.ShapeDtypeStruct((m, da), F32),
                   jax.ShapeDtypeStruct((m, da), F32), jax.ShapeDtypeStruct((m, da), BF16),
                   jax.ShapeDtypeStruct((xr.shape[0], n), F32)),
        grid=(n_tiles + 1,),
        in_specs=[row(d), const(shift), const(scale), const(xr), const(shift_r), const(scale_r), const(g_pre),
                  *w_specs, const(conv_w8), const(conv_b), const(g_cn), const(b_cn)],
        out_specs=(row(dc), pl.BlockSpec((CONV_HALO, dc), lambda i: (0, 0)),
                   row(da), row(da), row(da), row(da),
                   pl.BlockSpec((xr.shape[0], n), lambda i: (0, 0))),
        scratch_shapes=[pltpu.VMEM((bm + CONV_HALO, dc), F32),
                        pltpu.VMEM((SUBLANES - 1, conv_bt + CONV_HALO - SUBLANES, dc), F32),
                        pltpu.VMEM((bm, dc), F32)],
        compiler_params=_params("arbitrary"),
        name="in_proj",
    )(x, shift, scale, xr, shift_r, scale_r, g_pre, *([w_in_bf16] * len(segments)), conv_w8, conv_b, g_cn, b_cn)


def _conv_sample_kernel(state_ref, a_ref, b_ref, z_ref, w_ref, cb_ref, gcn_ref, bcn_ref,
                        o_ref, ns_ref, conv_ref):
    nb = state_ref.shape[0]
    w_hist = w_ref[0:CONV_STATE, :]
    u = a_ref[...] * _sigmoid(b_ref[...])
    for b in range(nb):
        st = state_ref[b]
        conv_ref[b:b + 1, :] = jnp.sum(st * w_hist, axis=0, keepdims=True)
        ns_ref[b, 0:CONV_STATE - 1, :] = state_ref[b, 1:CONV_STATE, :]
        ns_ref[b, CONV_STATE - 1:CONV_STATE, :] = u[b:b + 1, :]
    conv = conv_ref[...] + w_ref[CONV_STATE:CONV_WIDTH, :] * u + cb_ref[...]
    o_ref[...] = _ln_swish_gate(conv, gcn_ref, bcn_ref, _silu(z_ref[...])).astype(o_ref.dtype)


def _conv_sample(state, proj, conv_w, conv_b, g_cn, b_cn):
    nb, _, dc = state.shape
    const = lambda a: pl.BlockSpec(a.shape, lambda i: (0,) * a.ndim)
    col = lambda c: pl.BlockSpec((nb, dc), lambda i: (0, c))
    return pl.pallas_call(
        _conv_sample_kernel,
        out_shape=(jax.ShapeDtypeStruct((nb, dc), BF16),
                   jax.ShapeDtypeStruct((nb, CONV_STATE, dc), F32)),
        grid=(1,),
        in_specs=[const(state), col(0), col(1), col(2), const(conv_w), const(conv_b), const(g_cn), const(b_cn)],
        out_specs=(pl.BlockSpec((nb, dc), lambda i: (0, 0)),
                   pl.BlockSpec((nb, CONV_STATE, dc), lambda i: (0, 0, 0))),
        scratch_shapes=[pltpu.VMEM((nb, dc), F32)],
        compiler_params=_params("arbitrary"),
        name="conv_sample",
    )(state, proj, proj, proj, conv_w, conv_b, g_cn, b_cn)


def _lambda_full(lq1_ref, lk1_ref, lq2_ref, lk2_ref, layer_idx):
    s1 = jnp.sum(lq1_ref[...] * lk1_ref[...], axis=-1, keepdims=True)
    s2 = jnp.sum(lq2_ref[...] * lk2_ref[...], axis=-1, keepdims=True)
    return jnp.exp(s1) - jnp.exp(s2) + _lambda_init(layer_idx)


def _subln_gate(o, g_ref, gate, layer_idx):
    y = o * lax.rsqrt(jnp.mean(o * o, axis=-1, keepdims=True) + EPS) * g_ref[...]
    return y * (1.0 - _lambda_init(layer_idx)) * gate


def _trips_before(i):
    return (i * i) // 4


def _attn_kernel(pt_ref,
                 q_ref, k_ref, v_ref, ga_ref, g_ref, lq1_ref, lk1_ref, lq2_ref, lk2_ref,
                 qd_ref, knd_ref, vnd_ref, zd_ref, ck_hbm, cv_hbm,
                 o_ref, od_ref,
                 kb_ref, vt_ref, qs_ref, sa_ref, sb_ref, mxa_ref, mxb_ref, m_ref, l_ref, acc_ref,
                 kpg_ref, vpg_ref, ksem, vsem, md_ref, ld_ref, accd_ref,
                 *, bq, layer_idx, pages, trips_per_seq, n_trips):
    h = pl.program_id(0)
    i = pl.program_id(1)
    n_chunks = kb_ref.shape[0]
    hd = q_ref.shape[1]
    half = hd // 2
    nh = qd_ref.shape[1]
    n_slots = kpg_ref.shape[0]
    trips_per_head = _trips_before(n_chunks)

    def page_copies(trip, slot, from_table):
        copies = []
        for j in range(pages):
            page = pt_ref[trip * pages + j] if from_table else 0
            copies.append(pltpu.make_async_copy(ck_hbm.at[page], kpg_ref.at[slot, j], ksem.at[slot]))
            copies.append(pltpu.make_async_copy(cv_hbm.at[page], vpg_ref.at[slot, j], vsem.at[slot]))
        return copies

    def start_pages(trip, slot):
        for n, cp in enumerate(page_copies(trip, slot, True)):
            cp.start(priority=n % 2)

    @pl.when(jnp.logical_and(h == 0, i == 0))
    def _():
        for trip in range(n_slots - 1):
            start_pages(trip, trip)

    @pl.when(i == 0)
    def _():
        for c in range(n_chunks):
            kb_ref[c] = k_ref[c * bq:(c + 1) * bq, :].astype(BF16)
            vt_ref[c] = v_ref[c * bq:(c + 1) * bq, :].T.astype(BF16)

    q_t = q_ref[...].astype(F32).T
    d_idx = lax.broadcasted_iota(jnp.int32, q_t.shape, 0)
    qs_ref[:, 0:bq] = jnp.where(d_idx < half, q_t, 0.0).astype(BF16)
    qs_ref[:, bq:2 * bq] = jnp.where(d_idx >= half, q_t, 0.0).astype(BF16)
    m_ref[...] = jnp.full(m_ref.shape, -jnp.inf, F32)
    l_ref[...] = jnp.zeros(l_ref.shape, F32)
    acc_ref[...] = jnp.zeros(acc_ref.shape, F32)

    tile_w = MXU_TILE
    n_tiles = 2 * bq // tile_w

    def scores_tile(c, s_ref, mx_ref, t, diagonal=False):
        cs = slice(t * tile_w, (t + 1) * tile_w)
        s = jnp.dot(kb_ref[c], qs_ref[:, cs], preferred_element_type=F32)
        if diagonal:
            kidx = lax.broadcasted_iota(jnp.int32, s.shape, 0)
            qidx = (lax.broadcasted_iota(jnp.int32, s.shape, 1) + t * tile_w) & (bq - 1)
            s = jnp.where(kidx <= qidx, s, -jnp.inf)
        s_ref[:, cs] = s
        mx_ref[:, cs] = jnp.max(s, axis=0, keepdims=True)

    def accumulate_tile(c, s_ref, mx_ref, t):
        cs = slice(t * tile_w, (t + 1) * tile_w)
        m_old = m_ref[:, cs]
        m_new = jnp.maximum(m_old, mx_ref[:, cs])
        p = jnp.exp2(s_ref[:, cs] - m_new)
        alpha = jnp.exp2(m_old - m_new)
        l_ref[:, cs] = alpha * l_ref[:, cs] + jnp.sum(p, axis=0, keepdims=True)
        pv = jnp.dot(vt_ref[c], p.astype(BF16), preferred_element_type=F32)
        acc_ref[:, cs] = alpha * acc_ref[:, cs] + pv
        m_ref[:, cs] = m_new

    def decode_query(b):
        q = qd_ref[b] * half ** -0.5
        lane = lax.broadcasted_iota(jnp.int32, q.shape, 1)
        qx = jnp.concatenate([jnp.where(lane < half, q, 0.0), jnp.where(lane >= half, q, 0.0)], axis=0)
        return qx.astype(BF16)

    def decode_scores(b, slot):
        rows = pages * kpg_ref.shape[2]
        col_head = lax.broadcasted_iota(jnp.int32, (2 * nh, rows), 1) & (nh - 1)
        row_head = lax.broadcasted_iota(jnp.int32, (2 * nh, rows), 0) & (nh - 1)
        k_all = kpg_ref[slot].reshape(rows, hd).astype(BF16)
        s = lax.dot_general(decode_query(b), k_all, NT_DIMS, preferred_element_type=F32)
        return jnp.where(col_head == row_head, s, -jnp.inf)

    def decode_update(s, slot, first):
        m_old = jnp.where(first, -jnp.inf, md_ref[...])
        l_old = jnp.where(first, 0.0, ld_ref[...])
        acc_old = jnp.where(first, 0.0, accd_ref[...])
        m_new = jnp.maximum(m_old, jnp.max(s, axis=-1, keepdims=True))
        alpha = jnp.exp(m_old - m_new)
        p = jnp.exp(s - m_new)
        v_all = vpg_ref[slot].reshape(s.shape[1], hd).astype(BF16)
        md_ref[...] = m_new
        ld_ref[...] = alpha * l_old + jnp.sum(p, axis=-1, keepdims=True)
        accd_ref[...] = alpha * acc_old + jnp.dot(p.astype(BF16), v_all, preferred_element_type=F32)

    def decode_finish(b, lam):
        qx = decode_query(b).astype(F32)
        kn = knd_ref[b].astype(BF16).astype(F32)
        vn = vnd_ref[b].astype(BF16).astype(F32)
        kn2 = jnp.concatenate([kn, kn], axis=0)
        vn2 = jnp.concatenate([vn, vn], axis=0)
        s_new = jnp.sum(qx * kn2, axis=-1, keepdims=True)
        m_old = md_ref[...]
        m_fin = jnp.maximum(m_old, s_new)
        a_fin = jnp.exp(m_old - m_fin)
        p_new = jnp.exp(s_new - m_fin)
        l_fin = a_fin * ld_ref[...] + p_new
        on = (a_fin * accd_ref[...] + p_new * vn2) / l_fin
        o = on[0:nh] - lam * on[nh:2 * nh]
        od_ref[b] = _subln_gate(o, g_ref, _silu(zd_ref[b]), layer_idx)

    for t in range(n_tiles):
        scores_tile(i, sa_ref, mxa_ref, t, diagonal=True)
    trip0 = h * trips_per_head + _trips_before(i)

    def body(p, carry):
        trip = trip0 + p
        slot = lax.rem(trip, n_slots)
        for cp in page_copies(trip, slot, False):
            cp.wait()

        @pl.when(trip + n_slots - 1 < n_trips)
        def _():
            nxt = trip + n_slots - 1
            start_pages(nxt, lax.rem(nxt, n_slots))

        seq = trip // trips_per_seq
        part = lax.rem(trip, trips_per_seq)
        c_cur = jnp.where(p == 0, i, 2 * p - 1)
        for t in range(n_tiles):
            scores_tile(2 * p, sb_ref, mxb_ref, t)
            accumulate_tile(c_cur, sa_ref, mxa_ref, t)
        s_dec = decode_scores(seq, slot)
        c_next = jnp.minimum(2 * p + 1, i - 1)
        for t in range(n_tiles):
            scores_tile(c_next, sa_ref, mxa_ref, t)
            accumulate_tile(2 * p, sb_ref, mxb_ref, t)
        decode_update(s_dec, slot, part == 0)

        @pl.when(part == trips_per_seq - 1)
        def _():
            decode_finish(seq, _lambda_full(lq1_ref, lk1_ref, lq2_ref, lk2_ref, layer_idx))

        return carry

    lax.fori_loop(0, (i + 1) // 2, body, 0)

    @pl.when(i % 2 == 0)
    def _():
        for t in range(n_tiles):
            accumulate_tile(jnp.maximum(i - 1, 0), sa_ref, mxa_ref, t)

    lam = _lambda_full(lq1_ref, lk1_ref, lq2_ref, lk2_ref, layer_idx)
    acc = acc_ref[...]
    inv_l = 1.0 / l_ref[...]
    o_t = acc[:, 0:bq] * inv_l[:, 0:bq] - lam * (acc[:, bq:2 * bq] * inv_l[:, bq:2 * bq])
    o = o_t.T
    o_ref[...] = _subln_gate(o, g_ref, ga_ref[...].astype(F32), layer_idx).astype(o_ref.dtype)


def _attention(q, k, v, ga, g_subln, lq1, lk1, lq2, lk2,
               page_table, proj_d, att_col, cache_k, cache_v, layer_idx, bq=512, n_slots=4):
    s, da = q.shape
    hd = da // N_HEADS
    nq = s // bq
    nb, n_pages = page_table.shape
    n_pool, page, nh, _ = cache_k.shape
    ck = cache_k.reshape(n_pool, page * nh, hd)
    cv = cache_v.reshape(n_pool, page * nh, hd)
    pt = page_table.reshape(nb * n_pages)
    assert att_col % da == 0
    proj_heads = proj_d.reshape(nb, proj_d.shape[1] // hd, hd)
    first_blk = att_col // da

    n_trips = N_HEADS * _trips_before(nq)
    pages = (nb * n_pages) // n_trips
    assert pages * n_trips == nb * n_pages and n_pages % pages == 0, (n_trips, nb, n_pages)
    assert n_trips >= n_slots
    trips_per_seq = n_pages // pages

    kern = functools.partial(_attn_kernel, bq=bq, layer_idx=layer_idx, pages=pages,
                             trips_per_seq=trips_per_seq, n_trips=n_trips)
    tile = pl.BlockSpec((bq, hd), lambda h, i, pt: (i, h))
    full = pl.BlockSpec((s, hd), lambda h, i, pt: (0, h))
    small = lambda a: pl.BlockSpec(a.shape, lambda h, i, pt: (0, 0))
    whole = pl.BlockSpec((nb, nh, hd), lambda h, i, pt: (0, 0, 0))
    dec = lambda blk: pl.BlockSpec((nb, nh, hd), lambda h, i, pt: (0, first_blk + blk, 0))
    hbm = pl.BlockSpec(memory_space=pl.ANY)
    grid_spec = pltpu.PrefetchScalarGridSpec(
        num_scalar_prefetch=1,
        grid=(N_HEADS, nq),
        in_specs=[tile, full, full, tile, small(g_subln), small(lq1), small(lk1), small(lq2), small(lk2),
                  dec(0), dec(1), dec(2), dec(3), hbm, hbm],
        out_specs=(tile, whole),
        scratch_shapes=[pltpu.VMEM((nq, bq, hd), BF16),
                        pltpu.VMEM((nq, hd, bq), BF16),
                        pltpu.VMEM((hd, 2 * bq), BF16),
                        pltpu.VMEM((bq, 2 * bq), F32), pltpu.VMEM((bq, 2 * bq), F32),
                        pltpu.VMEM((1, 2 * bq), F32), pltpu.VMEM((1, 2 * bq), F32),
                        pltpu.VMEM((1, 2 * bq), F32), pltpu.VMEM((1, 2 * bq), F32),
                        pltpu.VMEM((hd, 2 * bq), F32),
                        pltpu.VMEM((n_slots, pages, page * nh, hd), F32),
                        pltpu.VMEM((n_slots, pages, page * nh, hd), F32),
                        pltpu.SemaphoreType.DMA((n_slots,)), pltpu.SemaphoreType.DMA((n_slots,)),
                        pltpu.VMEM((2 * nh, 1), F32), pltpu.VMEM((2 * nh, 1), F32),
                        pltpu.VMEM((2 * nh, hd), F32)],
    )
    return pl.pallas_call(
        kern,
        out_shape=(jax.ShapeDtypeStruct((s, da), BF16), jax.ShapeDtypeStruct((nb, nh, hd), F32)),
        grid_spec=grid_spec,
        compiler_params=_params("arbitrary", "arbitrary"),
        name="attention",
    )(pt, q, k, v, ga, g_subln, lq1, lk1, lq2, lk2,
      proj_heads, proj_heads, proj_heads, proj_heads, ck, cv)


def _out_kernel(oc_ref, oa_ref, wc_ref, wa_ref, x_ref, gate_ref, g_ref, o_ref):
    y = jnp.dot(oc_ref[...].astype(BF16), wc_ref[...], preferred_element_type=F32)
    y = y + jnp.dot(oa_ref[...].astype(BF16), wa_ref[...], preferred_element_type=F32)
    yn = y * lax.rsqrt(jnp.mean(y * y, axis=-1, keepdims=True) + EPS) * g_ref[...]
    o_ref[...] = x_ref[...] + gate_ref[...] * yn


def _out_proj(oc, oa, w_out_bf16, x, gate, g_post, bm):
    m, d = x.shape
    dc = oc.shape[1]
    da = oa.shape[1]
    per_row = gate.shape[0] != 1
    assert dc % da == 0
    gate_spec = pl.BlockSpec((bm, d), lambda i: (i, 0)) if per_row else pl.BlockSpec((1, d), lambda i: (0, 0))
    return pl.pallas_call(
        _out_kernel,
        out_shape=jax.ShapeDtypeStruct((m, d), F32),
        grid=(m // bm,),
        in_specs=[pl.BlockSpec((bm, dc), lambda i: (i, 0)),
                  pl.BlockSpec((bm, da), lambda i: (i, 0)),
                  pl.BlockSpec((dc, d), lambda i: (0, 0)),
                  pl.BlockSpec((da, d), lambda i: (dc // da, 0)),
                  pl.BlockSpec((bm, d), lambda i: (i, 0)),
                  gate_spec,
                  pl.BlockSpec((1, d), lambda i: (0, 0))],
        out_specs=pl.BlockSpec((bm, d), lambda i: (i, 0)),
        compiler_params=_params("arbitrary"),
        name="out_proj",
    )(oc, oa, w_out_bf16, w_out_bf16, x, gate, g_post)


def kernel(x_prompt, x_sample, cache_k, cache_v, state_conv, page_table, c_prompt, c_sample,
           w_ada, b_ada, g_pre, g_post, w_in, conv_w, conv_b, g_cn, b_cn,
           lq1, lk1, lq2, lk2, g_subln, w_out):
    depth = w_ada.shape[0]
    bp, seq, d = x_prompt.shape
    nb = x_sample.shape[0]
    assert bp == 1 and x_sample.shape[1] == 1
    dc = conv_w.shape[2]
    da = w_out.shape[1] - dc
    hd = da // N_HEADS
    dk = hd // 2

    xp = x_prompt.reshape(seq, d)
    xs = x_sample.reshape(nb, d)
    pad = (-(bp + nb)) % 8
    c_all = jnp.concatenate([c_prompt, c_sample, jnp.zeros((pad, d), F32)], axis=0)
    row = lambda a: a.reshape(1, -1)

    kp_l, vp_l, cp_l, ks_l, vs_l, cs_l = [], [], [], [], [], []
    for l in range(depth):
        w_in_b = w_in[l].astype(BF16)
        w_out_b = w_out[l].astype(BF16)
        mod = _modulation(c_all, w_ada[l], row(b_ada[l]))
        shift, scale, gate = mod[:, 0:d], mod[:, d:2 * d], mod[:, 2 * d:3 * d]
        vecs = (row(conv_b[l]), row(g_cn[l]), row(b_cn[l]))
        lams = (row(lq1[l]), row(lk1[l]), row(lq2[l]), row(lk2[l]))
        g_sub = row(g_subln[l])
        conv_w8 = jnp.broadcast_to(conv_w[l][:, None, :], (CONV_WIDTH, SUBLANES, dc))

        sl = slice(bp, bp + nb)
        oc, u_tail, q, k, v, ga, proj_s = _in_proj(
            xp, shift[0:1], scale[0:1], xs, shift[sl], scale[sl], row(g_pre[l]), w_in_b, conv_w8, *vecs,
            dc=dc, da=da, q_scale=dk ** -0.5 * math.log2(math.e))
        att_col = 3 * dc
        ksn = proj_s[:, att_col + da:att_col + 2 * da]
        vsn = proj_s[:, att_col + 2 * da:att_col + 3 * da]

        ocs, new_state = _conv_sample(state_conv[l], proj_s, conv_w[l], *vecs)

        oa, oas = _attention(q, k, v, ga, g_sub, *lams, page_table, proj_s, att_col,
                             cache_k[l], cache_v[l], layer_idx=l)

        xp = _out_proj(oc, oa, w_out_b, xp, gate[0:1], row(g_post[l]), bm=512)
        xs = _out_proj(ocs, oas.reshape(nb, da), w_out_b, xs, gate[sl], row(g_post[l]), bm=nb)
        kp_l.append(k.reshape(bp, seq, N_HEADS, hd))
        vp_l.append(v.reshape(bp, seq, N_HEADS, hd))
        cp_l.append(u_tail[CONV_HALO - CONV_STATE:].reshape(bp, CONV_STATE, dc))
        ks_l.append(ksn.reshape(nb, 1, N_HEADS, hd))
        vs_l.append(vsn.reshape(nb, 1, N_HEADS, hd))
        cs_l.append(new_state)

    return (xp.reshape(bp, seq, d), xs.reshape(nb, 1, d),
            jnp.stack(kp_l), jnp.stack(vp_l), jnp.stack(cp_l),
            jnp.stack(ks_l), jnp.stack(vs_l), jnp.stack(cs_l))
```
